```python
import math
import jax, jax.numpy as jnp
from jax import lax
import numpy as np

D_MODEL = 1024
BATCH = 8
SEQ = 2048
DEPTH = 4

CHUNK = 64
N_A = DEPTH // 2
N_B = DEPTH - N_A
CONV_W = 3
D_FF = 4 * D_MODEL
N_HEADS = 8
QK_NOPE = 128
QK_ROPE = 64
V_HEAD = 128
Q_LORA = 384
KV_LORA = 256
ROPE_THETA = 10000.0
Q_BLOCK = 128
NORM_EPS = 1e-6

kernel_name = "hybrid_shortconv_mla_yoco_trunk"


def rmsnorm(x, g):
    xf = x.astype(jnp.float32)
    y = xf * lax.rsqrt(jnp.mean(xf * xf, axis=-1, keepdims=True) + NORM_EPS)
    return (y * g.astype(jnp.float32)).astype(x.dtype)


def modulate(h, shift, scale):
    return h * (1 + scale[:, None, :]) + shift[:, None, :]


def rope_tables(seq_len, dim, dtype):
    inv_freq = 1.0 / (ROPE_THETA ** (jnp.arange(0, dim, 2, dtype=jnp.float32) / dim))
    ang = jnp.arange(seq_len, dtype=jnp.float32)[:, None] * inv_freq[None, :]
    return jnp.cos(ang).astype(dtype), jnp.sin(ang).astype(dtype)


def apply_rope(x, cos, sin):
    half = x.shape[-1] // 2
    x1, x2 = x[..., :half], x[..., half:]
    return jnp.concatenate([x1 * cos - x2 * sin, x2 * cos + x1 * sin], axis=-1)


def short_conv_mixer(h, w_in, w_conv, b_conv, w_out):
    bcx = h @ w_in
    b_gate, c_gate, xh = jnp.split(bcx, 3, axis=-1)
    u = c_gate * xh
    conv = lax.conv_general_dilated(
        u, w_conv[:, None, :], window_strides=(1,), padding=[(CONV_W - 1, 0)],
        dimension_numbers=("NWC", "WIO", "NWC"), feature_group_count=D_MODEL) + b_conv
    return (b_gate * conv) @ w_out


def squared_relu_mlp(h, w_up, w_down):
    return jnp.square(jax.nn.relu(h @ w_up)) @ w_down


def shared_kv(x, c_act, kv_ada_w, kv_ada_b, kv_norm_g, kv_a_w, kv_a_norm_g, kv_b_w, cos, sin):
    B, S, _ = x.shape
    shift, scale = jnp.split(c_act @ kv_ada_w + kv_ada_b, 2, axis=-1)
    hkv = modulate(rmsnorm(x, kv_norm_g), shift, scale)
    kv_a = hkv @ kv_a_w
    c_kv = rmsnorm(kv_a[..., :KV_LORA], kv_a_norm_g)
    k_pe = apply_rope(kv_a[..., KV_LORA:], cos, sin)
    kv = (c_kv @ kv_b_w).reshape(B, S, N_HEADS, QK_NOPE + V_HEAD)
    return kv[..., :QK_NOPE], kv[..., QK_NOPE:], k_pe


def mla_mixer(h, q_a_w, q_a_norm_g, q_b_w, o_w, k_nope, v, k_pe, cos, sin):
    B, S, _ = h.shape
    q = (rmsnorm(h @ q_a_w, q_a_norm_g) @ q_b_w).reshape(B, S, N_HEADS, QK_NOPE + QK_ROPE)
    sm_scale = 1.0 / math.sqrt(QK_NOPE + QK_ROPE)
    q_nope = q[..., :QK_NOPE] * sm_scale
    q_pe = apply_rope(q[..., QK_NOPE:], cos[:, None, :], sin[:, None, :]) * sm_scale
    outs = []
    for i in range(S // Q_BLOCK):
        q0, k_end = i * Q_BLOCK, (i + 1) * Q_BLOCK
        qn = q_nope[:, q0:k_end]
        qp = q_pe[:, q0:k_end]
        s = (jnp.einsum("bqhd,bkhd->bhqk", qn, k_nope[:, :k_end])
             + jnp.einsum("bqhr,bkr->bhqk", qp, k_pe[:, :k_end])).astype(jnp.float32)
        q_chunk = np.arange(q0, k_end) // CHUNK
        k_chunk = np.arange(k_end) // CHUNK
        mask = k_chunk[None, :] <= q_chunk[:, None]
        s = jnp.where(mask[None, None], s, jnp.finfo(jnp.float32).min)
        p = jax.nn.softmax(s, axis=-1).astype(v.dtype)
        outs.append(jnp.einsum("bhqk,bkhd->bqhd", p, v[:, :k_end]))
    o = jnp.concatenate(outs, axis=1).reshape(B, S, N_HEADS * V_HEAD)
    return o @ o_w


def setup_inputs(seed: int = 0) -> dict:
    key = jax.random.key(seed)
    ks = jax.random.split(key, 24)
    D = D_MODEL
    f32 = jnp.float32

    def w(k, shape, fan_in):
        return jax.random.normal(k, shape, f32) * (fan_in ** -0.5)

    def gain(k, shape):
        return 1.0 + 0.05 * jax.random.normal(k, shape, f32)

    return {
        "x": jax.random.normal(ks[0], (BATCH, SEQ, D), f32),
        "c": jax.random.normal(ks[1], (BATCH, D), f32),
        "ada_w": w(ks[2], (DEPTH, D, 6 * D), D) * 0.5,
        "ada_b": 0.02 * jax.random.normal(ks[3], (DEPTH, 6 * D), f32),
        "norm_g": gain(ks[4], (DEPTH, 4, D)),
        "conv_in_w": w(ks[5], (N_A, D, 3 * D), D),
        "conv_w": w(ks[6], (N_A, CONV_W, D), CONV_W),
        "conv_b": 0.02 * jax.random.normal(ks[7], (N_A, D), f32),
        "conv_out_w": w(ks[8], (N_A, D, D), D),
        "kv_ada_w": w(ks[9], (D, 2 * D), D) * 0.5,
        "kv_ada_b": 0.02 * jax.random.normal(ks[10], (2 * D,), f32),
        "kv_norm_g": gain(ks[11], (D,)),
        "kv_a_w": w(ks[12], (D, KV_LORA + QK_ROPE), D),
        "kv_a_norm_g": gain(ks[13], (KV_LORA,)),
        "kv_b_w": w(ks[14], (KV_LORA, N_HEADS * (QK_NOPE + V_HEAD)), KV_LORA),
        "q_a_w": w(ks[15], (N_B, D, Q_LORA), D),
        "q_a_norm_g": gain(ks[16], (N_B, Q_LORA)),
        "q_b_w": w(ks[17], (N_B, Q_LORA, N_HEADS * (QK_NOPE + QK_ROPE)), Q_LORA),
        "attn_o_w": w(ks[18], (N_B, N_HEADS * V_HEAD, D), N_HEADS * V_HEAD),
        "mlp_up_w": w(ks[19], (DEPTH, D, D_FF), D),
        "mlp_down_w": w(ks[20], (DEPTH, D_FF, D), D_FF),
    }


def reference(x, c, ada_w, ada_b, norm_g, conv_in_w, conv_w, conv_b, conv_out_w,
              kv_ada_w, kv_ada_b, kv_norm_g, kv_a_w, kv_a_norm_g, kv_b_w,
              q_a_w, q_a_norm_g, q_b_w, attn_o_w, mlp_up_w, mlp_down_w):
    S = x.shape[1]
    c_act = jax.nn.silu(c)
    cos, sin = rope_tables(S, QK_ROPE, x.dtype)
    k_nope = v = k_pe = None
    for l in range(DEPTH):
        mod = c_act @ ada_w[l] + ada_b[l]
        sh_m, sc_m, g_m, sh_f, sc_f, g_f = jnp.split(mod, 6, axis=-1)
        h = modulate(rmsnorm(x, norm_g[l, 0]), sh_m, sc_m)
        if l < N_A:
            y = short_conv_mixer(h, conv_in_w[l], conv_w[l], conv_b[l], conv_out_w[l])
        else:
            if l == N_A:
                k_nope, v, k_pe = shared_kv(x, c_act, kv_ada_w, kv_ada_b, kv_norm_g,
                                            kv_a_w, kv_a_norm_g, kv_b_w, cos, sin)
            j = l - N_A
            y = mla_mixer(h, q_a_w[j], q_a_norm_g[j], q_b_w[j], attn_o_w[j],
                          k_nope, v, k_pe, cos, sin)
        x = x + g_m[:, None, :] * rmsnorm(y, norm_g[l, 1])
        h = modulate(rmsnorm(x, norm_g[l, 2]), sh_f, sc_f)
        y = squared_relu_mlp(h, mlp_up_w[l], mlp_down_w[l])
        x = x + g_f[:, None, :] * rmsnorm(y, norm_g[l, 3])
    return x
```

```python
import functools
import math

import jax
import jax.numpy as jnp
from jax import lax
from jax.experimental import pallas as pl
from jax.experimental.pallas import tpu as pltpu

CHUNK = 64
CONV_W = 3
N_HEADS = 8
QK_NOPE = 128
QK_ROPE = 64
V_HEAD = 128
Q_LORA = 384
KV_LORA = 256
ROPE_THETA = 10000.0
NORM_EPS = 1e-6

V7X_VMEM_LIMIT_BYTES = 56 * 1024 * 1024
LANES = 128

TOKEN_TILE = 512
FF_CHUNK = 1024
ADA_COL_TILE = 1536
Q_TILE = 256

F32 = jnp.float32
BF16 = jnp.bfloat16


def _rms(x, g):
    ms = jnp.mean(x * x, axis=-1, keepdims=True)
    return x * lax.rsqrt(ms + NORM_EPS) * g


def _dot(a, b):
    return jnp.dot(a, b, preferred_element_type=F32)


def _dot_nt(a, b):
    return lax.dot_general(a, b, (((1,), (1,)), ((), ())),
                           preferred_element_type=F32)


def _const_spec(shape):
    nd = len(shape)
    return pl.BlockSpec(shape, lambda *_: (0,) * nd,
                        pipeline_mode=pl.Buffered(1))


def _params(n_axes):
    return pltpu.CompilerParams(
        dimension_semantics=("arbitrary",) * n_axes,
        vmem_limit_bytes=V7X_VMEM_LIMIT_BYTES)


def _ada_kernel(c_ref, w_ref, b_ref, o_ref):
    c = c_ref[...]
    c_act = (c * jax.nn.sigmoid(c)).astype(BF16)
    o_ref[0] = _dot(c_act, w_ref[0].astype(BF16)) + b_ref[0]


def _ada(c, w, b):
    n_layers, d, n = w.shape
    bsz = c.shape[0]
    tn = max(t for t in range(LANES, ADA_COL_TILE + 1, LANES) if n % t == 0)
    return pl.pallas_call(
        _ada_kernel,
        out_shape=jax.ShapeDtypeStruct((n_layers, bsz, n), F32),
        grid=(n_layers, n // tn),
        in_specs=[
            _const_spec((bsz, d)),
            pl.BlockSpec((1, d, tn), lambda l, j: (l, 0, j)),
            pl.BlockSpec((1, 1, tn), lambda l, j: (l, 0, j)),
        ],
        out_specs=pl.BlockSpec((1, bsz, tn), lambda l, j: (l, 0, j)),
        compiler_params=_params(2),
        name="ada_mod",
    )(c, w, b.reshape(n_layers, 1, n))


def _conv_kernel(x_ref, mod_ref, g_ref, win_ref, cw_ref, cb_ref, wout_ref,
                 o_ref, u_ref):
    tm, d = x_ref.shape[1], x_ref.shape[2]
    x = x_ref[0]
    mod = mod_ref[0]
    shift, scale, gate = mod[:, 0:d], mod[:, d:2 * d], mod[:, 2 * d:3 * d]
    h = _rms(x, g_ref[0:1, :]) * (1.0 + scale) + shift
    bcx = _dot(h.astype(BF16), win_ref[...])
    u = bcx[:, d:2 * d] * bcx[:, 2 * d:3 * d]

    @pl.when(pl.program_id(1) == 0)
    def _():
        u_ref[0:8, :] = jnp.zeros((8, d), F32)

    @pl.when(pl.program_id(1) > 0)
    def _():
        u_ref[0:8, :] = u_ref[tm:tm + 8, :]

    u_ref[8:8 + tm, :] = u
    conv = (cw_ref[0:1, :] * u_ref[6:6 + tm, :]
            + cw_ref[1:2, :] * u_ref[7:7 + tm, :]
            + cw_ref[2:3, :] * u
            + cb_ref[...])
    y = _dot((bcx[:, 0:d] * conv).astype(BF16), wout_ref[...])
    o_ref[0] = x + gate * _rms(y, g_ref[1:2, :])


def _conv_layer(x, mod, g, w_in, conv_w, conv_b, w_out):
    bsz, s, d = x.shape
    tm = TOKEN_TILE
    return pl.pallas_call(
        _conv_kernel,
        out_shape=jax.ShapeDtypeStruct(x.shape, F32),
        grid=(bsz, s // tm),
        in_specs=[
            pl.BlockSpec((1, tm, d), lambda b, i: (b, i, 0)),
            pl.BlockSpec((1, 1, 6 * d), lambda b, i: (b, 0, 0)),
            _const_spec(g.shape),
            _const_spec(w_in.shape),
            _const_spec(conv_w.shape),
            _const_spec((1, d)),
            _const_spec(w_out.shape),
        ],
        out_specs=pl.BlockSpec((1, tm, d), lambda b, i: (b, i, 0)),
        scratch_shapes=[pltpu.VMEM((tm + 8, d), F32)],
        compiler_params=_params(2),
        name="conv_mixer",
    )(x, mod, g, w_in, conv_w, conv_b.reshape(1, d), w_out)


def _mlp_kernel(*refs, with_oproj):
    if with_oproj:
        x_ref, mod_ref, g_ref, up_ref, down_ref, a_ref, ow_ref, o_ref = refs
    else:
        x_ref, mod_ref, g_ref, up_ref, down_ref, o_ref = refs
    d = x_ref.shape[2]
    d_ff = up_ref.shape[1]
    x = x_ref[0]
    mod = mod_ref[0]
    if with_oproj:
        y = _dot(a_ref[0], ow_ref[...])
        x = x + mod[:, 2 * d:3 * d] * _rms(y, g_ref[1:2, :])
    shift, scale, gate = mod[:, 3 * d:4 * d], mod[:, 4 * d:5 * d], mod[:, 5 * d:6 * d]
    h = (_rms(x, g_ref[2:3, :]) * (1.0 + scale) + shift).astype(BF16)
    y = None
    for k in range(0, d_ff, FF_CHUNK):
        hid = _dot(h, up_ref[:, k:k + FF_CHUNK])
        act = jnp.square(jnp.maximum(hid, 0.0)).astype(BF16)
        part = _dot(act, down_ref[k:k + FF_CHUNK, :])
        y = part if y is None else y + part
    o_ref[0] = x + gate * _rms(y, g_ref[3:4, :])


def _mlp_layer(x, mod, g, w_up, w_down, attn=None, w_o=None):
    bsz, s, d = x.shape
    tm = TOKEN_TILE
    with_oproj = attn is not None
    row_spec = pl.BlockSpec((1, tm, d), lambda b, i: (b, i, 0))
    in_specs = [
        row_spec,
        pl.BlockSpec((1, 1, 6 * d), lambda b, i: (b, 0, 0)),
        _const_spec(g.shape),
        _const_spec(w_up.shape),
        _const_spec(w_down.shape),
    ]
    args = [x, mod, g, w_up, w_down]
    if with_oproj:
        in_specs += [pl.BlockSpec((1, tm, attn.shape[2]), lambda b, i: (b, i, 0)),
                     _const_spec(w_o.shape)]
        args += [attn, w_o]
    return pl.pallas_call(
        functools.partial(_mlp_kernel, with_oproj=with_oproj),
        out_shape=jax.ShapeDtypeStruct(x.shape, F32),
        grid=(bsz, s // tm),
        in_specs=in_specs,
        out_specs=row_spec,
        compiler_params=_params(2),
        name="oproj_mlp" if with_oproj else "mlp",
    )(*args)


def _kv_kernel(x_ref, mod_ref, g_ref, wa_ref, ga_ref, wb_ref, cos_ref, sin_ref,
               kn_ref, v_ref, kpe_ref):
    d = x_ref.shape[2]
    n_k = kn_ref.shape[2]
    x = x_ref[0]
    mod = mod_ref[0]
    h = _rms(x, g_ref[...]) * (1.0 + mod[:, d:2 * d]) + mod[:, 0:d]
    kva = _dot(h.astype(BF16), wa_ref[...])
    c_kv = _rms(kva[:, 0:KV_LORA], ga_ref[...])
    pe = kva[:, KV_LORA:KV_LORA + LANES]
    kpe = pe * cos_ref[...] + pltpu.roll(pe, QK_ROPE, 1) * sin_ref[...]
    kpe_ref[0] = kpe.astype(BF16)
    kv = _dot(c_kv.astype(BF16), wb_ref[...])
    kn_ref[0] = kv[:, 0:n_k].astype(BF16)
    v_ref[0] = kv[:, n_k:].astype(BF16)


def _kv_side(x, mod, g, w_a, g_a, w_b, cos_t, sin_t):
    bsz, s, d = x.shape
    tm = TOKEN_TILE
    n_k, n_v = N_HEADS * QK_NOPE, N_HEADS * V_HEAD
    row = lambda n: pl.BlockSpec((1, tm, n), lambda b, i: (b, i, 0))
    return pl.pallas_call(
        _kv_kernel,
        out_shape=(jax.ShapeDtypeStruct((bsz, s, n_k), BF16),
                   jax.ShapeDtypeStruct((bsz, s, n_v), BF16),
                   jax.ShapeDtypeStruct((bsz, s, LANES), BF16)),
        grid=(bsz, s // tm),
        in_specs=[
            row(d),
            pl.BlockSpec((1, 1, 2 * d), lambda b, i: (b, 0, 0)),
            _const_spec((1, d)),
            _const_spec(w_a.shape),
            _const_spec((1, KV_LORA)),
            _const_spec(w_b.shape),
            pl.BlockSpec((tm, LANES), lambda b, i: (i, 0)),
            pl.BlockSpec((tm, LANES), lambda b, i: (i, 0)),
        ],
        out_specs=(row(n_k), row(n_v), row(LANES)),
        compiler_params=_params(2),
        name="kv_side",
    )(x, mod, g.reshape(1, d), w_a, g_a.reshape(1, KV_LORA), w_b, cos_t, sin_t)


def _q_kernel(x_ref, mod_ref, g_ref, wa_ref, ga_ref, wb_ref, cos_ref, sin_ref,
              q_ref):
    d = x_ref.shape[2]
    hw = 2 * LANES
    x = x_ref[0]
    mod = mod_ref[0]
    h = _rms(x, g_ref[0:1, :]) * (1.0 + mod[:, d:2 * d]) + mod[:, 0:d]
    qa = _rms(_dot(h.astype(BF16), wa_ref[...]), ga_ref[...])
    q = _dot(qa.astype(BF16), wb_ref[...])
    cos_t, sin_t = cos_ref[...], sin_ref[...]
    for hd in range(N_HEADS):
        blk = q[:, hd * hw:(hd + 1) * hw]
        out = blk * cos_t + pltpu.roll(blk, hw - QK_ROPE, 1) * sin_t
        q_ref[0, :, hd * hw:(hd + 1) * hw] = out.astype(BF16)


def _q_side(x, mod, g, w_a, g_a, w_b, cos_t, sin_t):
    bsz, s, d = x.shape
    tm = TOKEN_TILE
    n_q = N_HEADS * 2 * LANES
    return pl.pallas_call(
        _q_kernel,
        out_shape=jax.ShapeDtypeStruct((bsz, s, n_q), BF16),
        grid=(bsz, s // tm),
        in_specs=[
            pl.BlockSpec((1, tm, d), lambda b, i: (b, i, 0)),
            pl.BlockSpec((1, 1, 6 * d), lambda b, i: (b, 0, 0)),
            _const_spec(g.shape),
            _const_spec(w_a.shape),
            _const_spec((1, Q_LORA)),
            _const_spec(w_b.shape),
            pl.BlockSpec((tm, 2 * LANES), lambda b, i: (i, 0)),
            pl.BlockSpec((tm, 2 * LANES), lambda b, i: (i, 0)),
        ],
        out_specs=pl.BlockSpec((1, tm, n_q), lambda b, i: (b, i, 0)),
        compiler_params=_params(2),
        name="q_side",
    )(x, mod, g, w_a, g_a.reshape(1, Q_LORA), w_b, cos_t, sin_t)


def _attn_kernel(q_ref, kn_ref, kpe_ref, v_ref, o_ref):
    s = q_ref.shape[1]
    tq = Q_TILE
    k_cat = jnp.concatenate([kn_ref[0], kpe_ref[0]], axis=1)
    q_chunk = lax.broadcasted_iota(jnp.int32, (tq, tq), 0) // CHUNK
    k_chunk = lax.broadcasted_iota(jnp.int32, (tq, tq), 1) // CHUNK
    diag_mask = k_chunk <= q_chunk
    neg = jnp.finfo(F32).min
    for i in range(s // tq):
        lo, hi = i * tq, (i + 1) * tq
        q = q_ref[0, lo:hi, :]
        s_d = jnp.where(diag_mask, _dot_nt(q, k_cat[lo:hi]), neg)
        m = jnp.max(s_d, axis=-1, keepdims=True)
        if i > 0:
            s_o = _dot_nt(q, k_cat[0:lo])
            m = jnp.maximum(m, jnp.max(s_o, axis=-1, keepdims=True))
        p_d = jnp.exp(s_d - m)
        l = jnp.sum(p_d, axis=-1, keepdims=True)
        acc = _dot(p_d.astype(BF16), v_ref[0, lo:hi, :])
        if i > 0:
            p_o = jnp.exp(s_o - m)
            l = l + jnp.sum(p_o, axis=-1, keepdims=True)
            acc = acc + _dot(p_o.astype(BF16), v_ref[0, 0:lo, :])
        o_ref[0, lo:hi, :] = (acc / l).astype(BF16)


def _attention(q_cat, k_nope, k_pe, v):
    bsz, s, _ = q_cat.shape
    head = lambda n: pl.BlockSpec((1, s, n), lambda b, h: (b, 0, h))
    return pl.pallas_call(
        _attn_kernel,
        out_shape=jax.ShapeDtypeStruct((bsz, s, N_HEADS * V_HEAD), BF16),
        grid=(bsz, N_HEADS),
        in_specs=[
            head(2 * LANES),
            head(QK_NOPE),
            pl.BlockSpec((1, s, LANES), lambda b, h: (b, 0, 0)),
            head(V_HEAD),
        ],
        out_specs=head(V_HEAD),
        compiler_params=_params(2),
        name="mla_attention",
    )(q_cat, k_nope, k_pe, v)


def _half_swap(w):
    half = w.shape[-1] // 2
    return jnp.concatenate([w[..., half:], w[..., :half]], axis=-1)


def _rope_tables(seq_len):
    inv_freq = 1.0 / (ROPE_THETA ** (jnp.arange(0, QK_ROPE, 2, dtype=F32) / QK_ROPE))
    ang = jnp.arange(seq_len, dtype=F32)[:, None] * inv_freq[None, :]
    cos, sin = jnp.cos(ang), jnp.sin(ang)
    cos2 = jnp.concatenate([cos, cos], axis=-1)
    sin2 = jnp.concatenate([-sin, sin], axis=-1)
    zeros = jnp.zeros((seq_len, QK_ROPE), F32)
    k_cos = jnp.concatenate([cos2, zeros], axis=-1)
    k_sin = jnp.concatenate([sin2, zeros], axis=-1)
    sm_scale = 1.0 / math.sqrt(QK_NOPE + QK_ROPE)
    q_cos = jnp.concatenate([jnp.full((seq_len, QK_NOPE), sm_scale, F32),
                             cos2 * sm_scale, zeros], axis=-1)
    q_sin = jnp.concatenate([jnp.zeros((seq_len, QK_NOPE), F32),
                             sin2 * sm_scale, zeros], axis=-1)
    return k_cos, k_sin, q_cos, q_sin


def kernel(x, c, ada_w, ada_b, norm_g, conv_in_w, conv_w, conv_b, conv_out_w,
           kv_ada_w, kv_ada_b, kv_norm_g, kv_a_w, kv_a_norm_g, kv_b_w,
           q_a_w, q_a_norm_g, q_b_w, attn_o_w, mlp_up_w, mlp_down_w):
    bsz, s, d = x.shape
    depth = ada_w.shape[0]
    n_a = conv_in_w.shape[0]

    mod = _ada(c, ada_w, ada_b).reshape(depth, bsz, 1, 6 * d)
    kv_mod = _ada(c, kv_ada_w[None], kv_ada_b[None]).reshape(bsz, 1, 2 * d)
    k_cos, k_sin, q_cos, q_sin = _rope_tables(s)

    for l in range(n_a):
        x = _conv_layer(x, mod[l], norm_g[l], conv_in_w[l].astype(BF16),
                        conv_w[l], conv_b[l], conv_out_w[l].astype(BF16))
        x = _mlp_layer(x, mod[l], norm_g[l], mlp_up_w[l].astype(BF16),
                       mlp_down_w[l].astype(BF16))

    w_pe = kv_a_w[:, KV_LORA:]
    w_kva = jnp.concatenate([kv_a_w[:, :KV_LORA], w_pe, _half_swap(w_pe)], axis=1)
    w_kvb = kv_b_w.reshape(KV_LORA, N_HEADS, QK_NOPE + V_HEAD)
    w_kvb = jnp.concatenate([w_kvb[:, :, :QK_NOPE].reshape(KV_LORA, -1),
                             w_kvb[:, :, QK_NOPE:].reshape(KV_LORA, -1)], axis=1)
    k_nope, v, k_pe = _kv_side(x, kv_mod, kv_norm_g, w_kva.astype(BF16),
                               kv_a_norm_g, w_kvb.astype(BF16), k_cos, k_sin)

    for l in range(n_a, depth):
        j = l - n_a
        w_qb = q_b_w[j].reshape(Q_LORA, N_HEADS, QK_NOPE + QK_ROPE)
        w_qb = jnp.concatenate([w_qb, _half_swap(w_qb[:, :, QK_NOPE:])], axis=-1)
        w_qb = w_qb.reshape(Q_LORA, N_HEADS * 2 * LANES)
        q_cat = _q_side(x, mod[l], norm_g[l], q_a_w[j].astype(BF16),
                        q_a_norm_g[j], w_qb.astype(BF16), q_cos, q_sin)
        attn = _attention(q_cat, k_nope, k_pe, v)
        x = _mlp_layer(x, mod[l], norm_g[l], mlp_up_w[l].astype(BF16),
                       mlp_down_w[l].astype(BF16), attn, attn_o_w[j].astype(BF16))
    return x
```

```python
import functools
import math

import jax
import jax.numpy as jnp
from jax import lax
from jax.experimental import pallas as pl
from jax.experimental.pallas import tpu as pltpu

CHUNK = 64
CONV_W = 3
N_HEADS = 8
QK_NOPE = 128
QK_ROPE = 64
V_HEAD = 128
Q_LORA = 384
KV_LORA = 256
ROPE_THETA = 10000.0
NORM_EPS = 1e-6

V7X_VMEM_LIMIT_BYTES = 56 * 1024 * 1024
LANES = 128

TOKEN_TILE = 512
FF_CHUNK = 1024
ADA_COL_TILE = 1536
Q_TILE = 256

F32 = jnp.float32
BF16 = jnp.bfloat16


def _rms(x, g):
    ms = jnp.mean(x * x, axis=-1, keepdims=True)
    return x * lax.rsqrt(ms + NORM_EPS) * g


def _dot(a, b):
    return jnp.dot(a, b, preferred_element_type=F32)


def _dot_nt(a, b):
    return lax.dot_general(a, b, (((1,), (1,)), ((), ())),
                           preferred_element_type=F32)


def _const_spec(shape):
    nd = len(shape)
    return pl.BlockSpec(shape, lambda *_: (0,) * nd,
                        pipeline_mode=pl.Buffered(1))


def _params(n_axes):
    return pltpu.CompilerParams(
        dimension_semantics=("arbitrary",) * n_axes,
        vmem_limit_bytes=V7X_VMEM_LIMIT_BYTES)


def _ada_kernel(c_ref, w_ref, b_ref, o_ref):
    c = c_ref[...]
    c_act = (c * jax.nn.sigmoid(c)).astype(BF16)
    o_ref[0] = _dot(c_act, w_ref[0].astype(BF16)) + b_ref[0]


def _ada(c, w, b):
    n_layers, d, n = w.shape
    bsz = c.shape[0]
    tn = max(t for t in range(LANES, ADA_COL_TILE + 1, LANES) if n % t == 0)
    return pl.pallas_call(
        _ada_kernel,
        out_shape=jax.ShapeDtypeStruct((n_layers, bsz, n), F32),
        grid=(n_layers, n // tn),
        in_specs=[
            _const_spec((bsz, d)),
            pl.BlockSpec((1, d, tn), lambda l, j: (l, 0, j)),
            pl.BlockSpec((1, 1, tn), lambda l, j: (l, 0, j)),
        ],
        out_specs=pl.BlockSpec((1, bsz, tn), lambda l, j: (l, 0, j)),
        compiler_params=_params(2),
        name="ada_mod",
    )(c, w, b.reshape(n_layers, 1, n))


def _conv_kernel(x_ref, mod_ref, g_ref, win_ref, cw_ref, cb_ref, wout_ref,
                 o_ref, u_ref):
    tm, d = x_ref.shape[1], x_ref.shape[2]
    x = x_ref[0]
    mod = mod_ref[0]
    shift, scale, gate = mod[:, 0:d], mod[:, d:2 * d], mod[:, 2 * d:3 * d]
    h = _rms(x, g_ref[0:1, :]) * (1.0 + scale) + shift
    bcx = _dot(h.astype(BF16), win_ref[...])
    u = bcx[:, d:2 * d] * bcx[:, 2 * d:3 * d]

    @pl.when(pl.program_id(1) == 0)
    def _():
        u_ref[0:8, :] = jnp.zeros((8, d), F32)

    @pl.when(pl.program_id(1) > 0)
    def _():
        u_ref[0:8, :] = u_ref[tm:tm + 8, :]

    u_ref[8:8 + tm, :] = u
    conv = (cw_ref[0:1, :] * u_ref[6:6 + tm, :]
            + cw_ref[1:2, :] * u_ref[7:7 + tm, :]
            + cw_ref[2:3, :] * u
            + cb_ref[...])
    y = _dot((bcx[:, 0:d] * conv).astype(BF16), wout_ref[...])
    o_ref[0] = x + gate * _rms(y, g_ref[1:2, :])


def _conv_layer(x, mod, g, w_in, conv_w, conv_b, w_out):
    bsz, s, d = x.shape
    tm = TOKEN_TILE
    return pl.pallas_call(
        _conv_kernel,
        out_shape=jax.ShapeDtypeStruct(x.shape, F32),
        grid=(bsz, s // tm),
        in_specs=[
            pl.BlockSpec((1, tm, d), lambda b, i: (b, i, 0)),
            pl.BlockSpec((1, 1, 6 * d), lambda b, i: (b, 0, 0)),
            _const_spec(g.shape),
            _const_spec(w_in.shape),
            _const_spec(conv_w.shape),
            _const_spec((1, d)),
            _const_spec(w_out.shape),
        ],
        out_specs=pl.BlockSpec((1, tm, d), lambda b, i: (b, i, 0)),
        scratch_shapes=[pltpu.VMEM((tm + 8, d), F32)],
        compiler_params=_params(2),
        name="conv_mixer",
    )(x, mod, g, w_in, conv_w, conv_b.reshape(1, d), w_out)


def _mlp_kernel(*refs, with_oproj):
    if with_oproj:
        x_ref, mod_ref, g_ref, up_ref, down_ref, a_ref, ow_ref, o_ref = refs
    else:
        x_ref, mod_ref, g_ref, up_ref, down_ref, o_ref = refs
    d = x_ref.shape[2]
    d_ff = up_ref.shape[1]
    x = x_ref[0]
    mod = mod_ref[0]
    if with_oproj:
        y = _dot(a_ref[0], ow_ref[...])
        x = x + mod[:, 2 * d:3 * d] * _rms(y, g_ref[1:2, :])
    shift, scale, gate = mod[:, 3 * d:4 * d], mod[:, 4 * d:5 * d], mod[:, 5 * d:6 * d]
    h = (_rms(x, g_ref[2:3, :]) * (1.0 + scale) + shift).astype(BF16)
    y = None
    for k in range(0, d_ff, FF_CHUNK):
        hid = _dot(h, up_ref[:, k:k + FF_CHUNK])
        act = jnp.square(jnp.maximum(hid, 0.0)).astype(BF16)
        part = _dot(act, down_ref[k:k + FF_CHUNK, :])
        y = part if y is None else y + part
    o_ref[0] = x + gate * _rms(y, g_ref[3:4, :])


def _mlp_layer(x, mod, g, w_up, w_down, attn=None, w_o=None):
    bsz, s, d = x.shape
    tm = TOKEN_TILE
    with_oproj = attn is not None
    row_spec = pl.BlockSpec((1, tm, d), lambda b, i: (b, i, 0))
    in_specs = [
        row_spec,
        pl.BlockSpec((1, 1, 6 * d), lambda b, i: (b, 0, 0)),
        _const_spec(g.shape),
        _const_spec(w_up.shape),
        _const_spec(w_down.shape),
    ]
    args = [x, mod, g, w_up, w_down]
    if with_oproj:
        in_specs += [pl.BlockSpec((1, tm, attn.shape[2]), lambda b, i: (b, i, 0)),
                     _const_spec(w_o.shape)]
        args += [attn, w_o]
    return pl.pallas_call(
        functools.partial(_mlp_kernel, with_oproj=with_oproj),
        out_shape=jax.ShapeDtypeStruct(x.shape, F32),
        grid=(bsz, s // tm),
        in_specs=in_specs,
        out_specs=row_spec,
        compiler_params=_params(2),
        name="oproj_mlp" if with_oproj else "mlp",
    )(*args)


def _kv_kernel(x_ref, mod_ref, g_ref, wa_ref, ga_ref, wk_ref, wvt_ref,
               cos_ref, sin_ref, kn_ref, vt_ref, kpe_ref):
    d = x_ref.shape[2]
    x = x_ref[0]
    mod = mod_ref[0]
    h = _rms(x, g_ref[...]) * (1.0 + mod[:, d:2 * d]) + mod[:, 0:d]
    kva = _dot(h.astype(BF16), wa_ref[...])
    c_kv = _rms(kva[:, 0:KV_LORA], ga_ref[...])
    pe = kva[:, KV_LORA:KV_LORA + LANES]
    kpe = pe * cos_ref[...] + pltpu.roll(pe, QK_ROPE, 1) * sin_ref[...]
    kpe_ref[0] = kpe.astype(BF16)
    c_kv = c_kv.astype(BF16)
    kn_ref[0] = _dot(c_kv, wk_ref[...]).astype(BF16)
    vt_ref[0] = _dot_nt(wvt_ref[...], c_kv).astype(BF16)


def _kv_side(x, mod, g, w_a, g_a, w_k, w_vt, cos_t, sin_t):
    bsz, s, d = x.shape
    tm = TOKEN_TILE
    n_k, n_v = w_k.shape[1], w_vt.shape[0]
    row = lambda n: pl.BlockSpec((1, tm, n), lambda b, i: (b, i, 0))
    return pl.pallas_call(
        _kv_kernel,
        out_shape=(jax.ShapeDtypeStruct((bsz, s, n_k), BF16),
                   jax.ShapeDtypeStruct((bsz, n_v, s), BF16),
                   jax.ShapeDtypeStruct((bsz, s, LANES), BF16)),
        grid=(bsz, s // tm),
        in_specs=[
            row(d),
            pl.BlockSpec((1, 1, 2 * d), lambda b, i: (b, 0, 0)),
            _const_spec((1, d)),
            _const_spec(w_a.shape),
            _const_spec((1, KV_LORA)),
            _const_spec(w_k.shape),
            _const_spec(w_vt.shape),
            pl.BlockSpec((tm, LANES), lambda b, i: (i, 0)),
            pl.BlockSpec((tm, LANES), lambda b, i: (i, 0)),
        ],
        out_specs=(row(n_k),
                   pl.BlockSpec((1, n_v, tm), lambda b, i: (b, 0, i)),
                   row(LANES)),
        compiler_params=_params(2),
        name="kv_side",
    )(x, mod, g.reshape(1, d), w_a, g_a.reshape(1, KV_LORA), w_k, w_vt,
      cos_t, sin_t)


def _q_kernel(x_ref, mod_ref, g_ref, wa_ref, ga_ref, wb_ref, cos_ref, sin_ref,
              q_ref):
    d = x_ref.shape[2]
    hw = 2 * LANES
    x = x_ref[0]
    mod = mod_ref[0]
    h = _rms(x, g_ref[0:1, :]) * (1.0 + mod[:, d:2 * d]) + mod[:, 0:d]
    qa = _rms(_dot(h.astype(BF16), wa_ref[...]), ga_ref[...])
    q = _dot(qa.astype(BF16), wb_ref[...])
    cos_t, sin_t = cos_ref[...], sin_ref[...]
    for hd in range(N_HEADS):
        blk = q[:, hd * hw:(hd + 1) * hw]
        out = blk * cos_t + pltpu.roll(blk, hw - QK_ROPE, 1) * sin_t
        q_ref[0, :, hd * hw:(hd + 1) * hw] = out.astype(BF16)


def _q_side(x, mod, g, w_a, g_a, w_b, cos_t, sin_t):
    bsz, s, d = x.shape
    tm = TOKEN_TILE
    n_q = N_HEADS * 2 * LANES
    return pl.pallas_call(
        _q_kernel,
        out_shape=jax.ShapeDtypeStruct((bsz, s, n_q), BF16),
        grid=(bsz, s // tm),
        in_specs=[
            pl.BlockSpec((1, tm, d), lambda b, i: (b, i, 0)),
            pl.BlockSpec((1, 1, 6 * d), lambda b, i: (b, 0, 0)),
            _const_spec(g.shape),
            _const_spec(w_a.shape),
            _const_spec((1, Q_LORA)),
            _const_spec(w_b.shape),
            pl.BlockSpec((tm, 2 * LANES), lambda b, i: (i, 0)),
            pl.BlockSpec((tm, 2 * LANES), lambda b, i: (i, 0)),
        ],
        out_specs=pl.BlockSpec((1, tm, n_q), lambda b, i: (b, i, 0)),
        compiler_params=_params(2),
        name="q_side",
    )(x, mod, g, w_a, g_a.reshape(1, Q_LORA), w_b, cos_t, sin_t)


def _col_reduce(x, op, final):
    n = x.shape[0]
    while n > 8 and n % 16 == 0:
        n //= 2
        x = op(x[:n], x[n:])
    return final(x, axis=0, keepdims=True)


def _attn_kernel(q_ref, kn_ref, kpe_ref, vt_ref, o_ref, kc_ref, s_ref, p_ref):
    s = q_ref.shape[1]
    tq = Q_TILE
    kc_ref[:, 0:LANES] = kn_ref[0]
    kc_ref[:, LANES:2 * LANES] = kpe_ref[0]
    k_chunk = lax.broadcasted_iota(jnp.int32, (tq, tq), 0) // CHUNK
    q_chunk = lax.broadcasted_iota(jnp.int32, (tq, tq), 1) // CHUNK
    diag_mask = k_chunk <= q_chunk
    neg = jnp.finfo(F32).min
    n_tiles = s // tq

    def scores(i, slot):
        lo, hi = i * tq, (i + 1) * tq
        q = q_ref[0, lo:hi, :]
        if i > 0:
            s_ref[slot, 0:lo, :] = _dot_nt(kc_ref[0:lo, :], q)
        s_ref[slot, lo:hi, :] = jnp.where(
            diag_mask, _dot_nt(kc_ref[lo:hi, :], q), neg)

    def softmax_pv(i, slot):
        lo, hi = i * tq, (i + 1) * tq
        blocks = [(j * tq, (j + 1) * tq) for j in range(i + 1)]
        m = None
        for a, b in blocks:
            mb = _col_reduce(s_ref[slot, a:b, :], jnp.maximum, jnp.max)
            m = mb if m is None else jnp.maximum(m, mb)
        l = None
        for a, b in blocks:
            p = jnp.exp2(s_ref[slot, a:b, :] - m)
            lb = _col_reduce(p, jnp.add, jnp.sum)
            l = lb if l is None else l + lb
            p_ref[slot, a:b, :] = p.astype(BF16)
        acc = _dot(vt_ref[0, :, 0:hi], p_ref[slot, 0:hi, :])
        o_ref[0, lo:hi, :] = (acc * (1.0 / l)).T.astype(BF16)

    order = list(range(1, n_tiles, 2)) + list(range(n_tiles - 2 + n_tiles % 2, -1, -2))
    scores(order[0], 0)
    for pos, i in enumerate(order):
        if pos + 1 < n_tiles:
            scores(order[pos + 1], (pos + 1) % 2)
        softmax_pv(i, pos % 2)


def _attention(q_cat, k_nope, k_pe, v_t):
    bsz, s, _ = q_cat.shape
    head = lambda n: pl.BlockSpec((1, s, n), lambda b, h: (b, 0, h))
    return pl.pallas_call(
        _attn_kernel,
        out_shape=jax.ShapeDtypeStruct((bsz, s, N_HEADS * V_HEAD), BF16),
        grid=(bsz, N_HEADS),
        in_specs=[
            head(2 * LANES),
            head(QK_NOPE),
            pl.BlockSpec((1, s, LANES), lambda b, h: (b, 0, 0)),
            pl.BlockSpec((1, V_HEAD, s), lambda b, h: (b, h, 0)),
        ],
        out_specs=head(V_HEAD),
        scratch_shapes=[pltpu.VMEM((s, 2 * LANES), BF16),
                        pltpu.VMEM((2, s, Q_TILE), F32),
                        pltpu.VMEM((2, s, Q_TILE), BF16)],
        compiler_params=_params(2),
        name="mla_attention",
    )(q_cat, k_nope, k_pe, v_t)


def _half_swap(w):
    half = w.shape[-1] // 2
    return jnp.concatenate([w[..., half:], w[..., :half]], axis=-1)


def _rope_tables(seq_len):
    inv_freq = 1.0 / (ROPE_THETA ** (jnp.arange(0, QK_ROPE, 2, dtype=F32) / QK_ROPE))
    ang = jnp.arange(seq_len, dtype=F32)[:, None] * inv_freq[None, :]
    cos, sin = jnp.cos(ang), jnp.sin(ang)
    cos2 = jnp.concatenate([cos, cos], axis=-1)
    sin2 = jnp.concatenate([-sin, sin], axis=-1)
    zeros = jnp.zeros((seq_len, QK_ROPE), F32)
    k_cos = jnp.concatenate([cos2, zeros], axis=-1)
    k_sin = jnp.concatenate([sin2, zeros], axis=-1)
    sm_scale = math.log2(math.e) / math.sqrt(QK_NOPE + QK_ROPE)
    q_cos = jnp.concatenate([jnp.full((seq_len, QK_NOPE), sm_scale, F32),
                             cos2 * sm_scale, zeros], axis=-1)
    q_sin = jnp.concatenate([jnp.zeros((seq_len, QK_NOPE), F32),
                             sin2 * sm_scale, zeros], axis=-1)
    return k_cos, k_sin, q_cos, q_sin


def kernel(x, c, ada_w, ada_b, norm_g, conv_in_w, conv_w, conv_b, conv_out_w,
           kv_ada_w, kv_ada_b, kv_norm_g, kv_a_w, kv_a_norm_g, kv_b_w,
           q_a_w, q_a_norm_g, q_b_w, attn_o_w, mlp_up_w, mlp_down_w):
    bsz, s, d = x.shape
    depth = ada_w.shape[0]
    n_a = conv_in_w.shape[0]

    mod = _ada(c, ada_w, ada_b).reshape(depth, bsz, 1, 6 * d)
    kv_mod = _ada(c, kv_ada_w[None], kv_ada_b[None]).reshape(bsz, 1, 2 * d)
    k_cos, k_sin, q_cos, q_sin = _rope_tables(s)

    for l in range(n_a):
        x = _conv_layer(x, mod[l], norm_g[l], conv_in_w[l].astype(BF16),
                        conv_w[l], conv_b[l], conv_out_w[l].astype(BF16))
        x = _mlp_layer(x, mod[l], norm_g[l], mlp_up_w[l].astype(BF16),
                       mlp_down_w[l].astype(BF16))

    w_pe = kv_a_w[:, KV_LORA:]
    w_kva = jnp.concatenate([kv_a_w[:, :KV_LORA], w_pe, _half_swap(w_pe)], axis=1)
    w_kvb = kv_b_w.reshape(KV_LORA, N_HEADS, QK_NOPE + V_HEAD)
    w_k = w_kvb[:, :, :QK_NOPE].reshape(KV_LORA, -1)
    w_vt = w_kvb[:, :, QK_NOPE:].reshape(KV_LORA, -1).T
    k_nope, v_t, k_pe = _kv_side(x, kv_mod, kv_norm_g, w_kva.astype(BF16),
                                 kv_a_norm_g, w_k.astype(BF16),
                                 w_vt.astype(BF16), k_cos, k_sin)

    for l in range(n_a, depth):
        j = l - n_a
        w_qb = q_b_w[j].reshape(Q_LORA, N_HEADS, QK_NOPE + QK_ROPE)
        w_qb = jnp.concatenate([w_qb, _half_swap(w_qb[:, :, QK_NOPE:])], axis=-1)
        w_qb = w_qb.reshape(Q_LORA, N_HEADS * 2 * LANES)
        q_cat = _q_side(x, mod[l], norm_g[l], q_a_w[j].astype(BF16),
                        q_a_norm_g[j], w_qb.astype(BF16), q_cos, q_sin)
        attn = _attention(q_cat, k_nope, k_pe, v_t)
        x = _mlp_layer(x, mod[l], norm_g[l], mlp_up_w[l].astype(BF16),
                       mlp_down_w[l].astype(BF16), attn, attn_o_w[j].astype(BF16))
    return x
```

```python
import functools
import math

import jax
import jax.numpy as jnp
from jax import lax
from jax.experimental import pallas as pl
from jax.experimental.pallas import tpu as pltpu

CHUNK = 64
CONV_W = 3
N_HEADS = 8
QK_NOPE = 128
QK_ROPE = 64
V_HEAD = 128
Q_LORA = 384
KV_LORA = 256
ROPE_THETA = 10000.0
NORM_EPS = 1e-6

V7X_VMEM_LIMIT_BYTES = 56 * 1024 * 1024
LANES = 128

TOKEN_TILE = 512
FF_CHUNK = 1024
ADA_COL_TILE = 1536
Q_TILE = 256
ROW_SUB = 256

F32 = jnp.float32
BF16 = jnp.bfloat16


def _rms(x, g):
    ms = jnp.mean(x * x, axis=-1, keepdims=True)
    return x * lax.rsqrt(ms + NORM_EPS) * g


def _dot(a, b):
    return jnp.dot(a, b, preferred_element_type=F32)


def _dot_nt(a, b):
    return lax.dot_general(a, b, (((1,), (1,)), ((), ())),
                           preferred_element_type=F32)


def _const_spec(shape):
    nd = len(shape)
    return pl.BlockSpec(shape, lambda *_: (0,) * nd,
                        pipeline_mode=pl.Buffered(1))


def _params(n_axes):
    return pltpu.CompilerParams(
        dimension_semantics=("arbitrary",) * n_axes,
        vmem_limit_bytes=V7X_VMEM_LIMIT_BYTES)


def _ada_kernel(c_ref, w_ref, b_ref, o_ref):
    c = c_ref[...]
    c_act = (c * jax.nn.sigmoid(c)).astype(BF16)
    o_ref[0] = _dot(c_act, w_ref[0].astype(BF16)) + b_ref[0]


def _ada(c, w, b):
    n_layers, d, n = w.shape
    bsz = c.shape[0]
    tn = max(t for t in range(LANES, ADA_COL_TILE + 1, LANES) if n % t == 0)
    return pl.pallas_call(
        _ada_kernel,
        out_shape=jax.ShapeDtypeStruct((n_layers, bsz, n), F32),
        grid=(n_layers, n // tn),
        in_specs=[
            _const_spec((bsz, d)),
            pl.BlockSpec((1, d, tn), lambda l, j: (l, 0, j)),
            pl.BlockSpec((1, 1, tn), lambda l, j: (l, 0, j)),
        ],
        out_specs=pl.BlockSpec((1, bsz, tn), lambda l, j: (l, 0, j)),
        compiler_params=_params(2),
        name="ada_mod",
    )(c, w, b.reshape(n_layers, 1, n))


def _conv_kernel(x_ref, mod_ref, g_ref, win_ref, cw_ref, cb_ref, wout_ref,
                 o_ref, u_ref):
    tm, d = x_ref.shape[1], x_ref.shape[2]
    rs = ROW_SUB
    mod = mod_ref[0]
    shift, scale, gate = mod[:, 0:d], mod[:, d:2 * d], mod[:, 2 * d:3 * d]

    @pl.when(pl.program_id(1) == 0)
    def _():
        u_ref[0:8, :] = jnp.zeros((8, d), F32)

    @pl.when(pl.program_id(1) > 0)
    def _():
        u_ref[0:8, :] = u_ref[tm:tm + 8, :]

    def project(r):
        x = x_ref[0, r * rs:(r + 1) * rs, :]
        h = _rms(x, g_ref[0:1, :]) * (1.0 + scale) + shift
        return _dot(h.astype(BF16), win_ref[...])

    def mix(r, bcx):
        lo = 8 + r * rs
        u = bcx[:, d:2 * d] * bcx[:, 2 * d:3 * d]
        u_ref[lo:lo + rs, :] = u
        conv = (cw_ref[0:1, :] * u_ref[lo - 2:lo - 2 + rs, :]
                + cw_ref[1:2, :] * u_ref[lo - 1:lo - 1 + rs, :]
                + cw_ref[2:3, :] * u
                + cb_ref[...])
        y = _dot((bcx[:, 0:d] * conv).astype(BF16), wout_ref[...])
        rows = slice(r * rs, (r + 1) * rs)
        o_ref[0, rows, :] = x_ref[0, rows, :] + gate * _rms(y, g_ref[1:2, :])

    nxt = project(0)
    for r in range(tm // rs):
        cur = nxt
        if r + 1 < tm // rs:
            nxt = project(r + 1)
        mix(r, cur)


def _conv_layer(x, mod, g, w_in, conv_w, conv_b, w_out):
    bsz, s, d = x.shape
    tm = TOKEN_TILE
    return pl.pallas_call(
        _conv_kernel,
        out_shape=jax.ShapeDtypeStruct(x.shape, F32),
        grid=(bsz, s // tm),
        in_specs=[
            pl.BlockSpec((1, tm, d), lambda b, i: (b, i, 0)),
            pl.BlockSpec((1, 1, 6 * d), lambda b, i: (b, 0, 0)),
            _const_spec(g.shape),
            _const_spec(w_in.shape),
            _const_spec(conv_w.shape),
            _const_spec((1, d)),
            _const_spec(w_out.shape),
        ],
        out_specs=pl.BlockSpec((1, tm, d), lambda b, i: (b, i, 0)),
        scratch_shapes=[pltpu.VMEM((tm + 8, d), F32)],
        compiler_params=_params(2),
        name="conv_mixer",
    )(x, mod, g, w_in, conv_w, conv_b.reshape(1, d), w_out)


def _kv_tail(x, rows, mod_ref, g_ref, wa_ref, ga_ref, wk_ref, wvt_ref,
             cos_ref, sin_ref, kn_ref, vt_ref, kpe_ref):
    d = x.shape[1]
    mod = mod_ref[0]
    h = _rms(x, g_ref[...]) * (1.0 + mod[:, d:2 * d]) + mod[:, 0:d]
    kva = _dot(h.astype(BF16), wa_ref[...])
    c_kv = _rms(kva[:, 0:KV_LORA], ga_ref[...]).astype(BF16)
    pe = kva[:, KV_LORA:KV_LORA + LANES]
    kpe = pe * cos_ref[rows, :] + pltpu.roll(pe, QK_ROPE, 1) * sin_ref[rows, :]
    kpe_ref[0, rows, :] = kpe.astype(BF16)
    kn = _dot(c_kv, wk_ref[...]).astype(BF16)
    for hd in range(N_HEADS):
        kn_ref[0, hd, rows, :] = kn[:, hd * QK_NOPE:(hd + 1) * QK_NOPE]
    vt_ref[0, :, rows] = _dot_nt(wvt_ref[...], c_kv).astype(BF16)


def _q_tail(x, rows, mod_ref, g_ref, wa_ref, ga_ref, wb_ref, cos_ref, sin_ref,
            q_ref):
    d = x.shape[1]
    hw = 2 * LANES
    mod = mod_ref[0]
    h = _rms(x, g_ref[0:1, :]) * (1.0 + mod[:, d:2 * d]) + mod[:, 0:d]
    qa = _rms(_dot(h.astype(BF16), wa_ref[...]), ga_ref[...])
    q = _dot(qa.astype(BF16), wb_ref[...])
    cos_t, sin_t = cos_ref[rows, :], sin_ref[rows, :]
    for hd in range(N_HEADS):
        blk = q[:, hd * hw:(hd + 1) * hw]
        out = blk * cos_t + pltpu.roll(blk, hw - QK_ROPE, 1) * sin_t
        q_ref[0, hd, rows, :] = out.astype(BF16)


N_KV_IN, N_KV_OUT = 8, 3
N_Q_IN, N_Q_OUT = 7, 1


def _mlp_kernel(*refs, with_oproj, with_kv, with_q):
    refs = list(refs)
    x_ref, mod_ref, g_ref, up_ref, down_ref = refs[:5]
    pos = 5
    if with_oproj:
        a_ref, ow_ref = refs[pos:pos + 2]
        pos += 2
    if with_kv:
        kv_in = refs[pos:pos + N_KV_IN]
        pos += N_KV_IN
    if with_q:
        q_in = refs[pos:pos + N_Q_IN]
        pos += N_Q_IN
    o_ref = refs[pos]
    pos += 1
    if with_kv:
        kv_out = refs[pos:pos + N_KV_OUT]
        pos += N_KV_OUT
    if with_q:
        q_out = refs[pos:pos + N_Q_OUT]

    tm, d = x_ref.shape[1], x_ref.shape[2]
    d_ff = up_ref.shape[1]
    rs = ROW_SUB
    n_sub = tm // rs
    mod = mod_ref[0]
    gate_m = mod[:, 2 * d:3 * d]
    shift, scale, gate = mod[:, 3 * d:4 * d], mod[:, 4 * d:5 * d], mod[:, 5 * d:6 * d]
    chunks = list(range(0, d_ff, FF_CHUNK))

    def prologue(r):
        rows = slice(r * rs, (r + 1) * rs)
        x = x_ref[0, rows, :]
        if with_oproj:
            a = jnp.concatenate([a_ref[0, hd, rows, :] for hd in range(N_HEADS)],
                                axis=1)
            y = _dot(a, ow_ref[...])
            x = x + gate_m * _rms(y, g_ref[1:2, :])
        h = (_rms(x, g_ref[2:3, :]) * (1.0 + scale) + shift).astype(BF16)
        return x, h

    def ff_chunk(h, k, y):
        hid = _dot(h, up_ref[:, k:k + FF_CHUNK])
        act = jnp.square(jnp.maximum(hid, 0.0)).astype(BF16)
        part = _dot(act, down_ref[k:k + FF_CHUNK, :])
        return part if y is None else y + part

    def epilogue(r, x, y):
        rows = slice(r * rs, (r + 1) * rs)
        x_new = x + gate * _rms(y, g_ref[3:4, :])
        o_ref[0, rows, :] = x_new
        if with_kv:
            _kv_tail(x_new, rows, *kv_in, *kv_out)
        if with_q:
            _q_tail(x_new, rows, *q_in, *q_out)

    x, h = prologue(0)
    pending = None
    for r in range(n_sub):
        y = None
        for ci, k in enumerate(chunks):
            if ci == len(chunks) - 1 and r + 1 < n_sub:
                nxt = prologue(r + 1)
            y = ff_chunk(h, k, y)
            if ci == 0 and pending is not None:
                epilogue(*pending)
                pending = None
        pending = (r, x, y)
        if r + 1 < n_sub:
            x, h = nxt
    epilogue(*pending)


def _mlp_layer(x, mod, g, w_up, w_down, oproj=None, kv=None, q=None):
    bsz, s, d = x.shape
    tm = TOKEN_TILE
    row = lambda n: pl.BlockSpec((1, tm, n), lambda b, i: (b, i, 0))
    vec = lambda n: pl.BlockSpec((1, 1, n), lambda b, i: (b, 0, 0))
    table = lambda n: pl.BlockSpec((tm, n), lambda b, i: (i, 0))
    heads = lambda n: pl.BlockSpec((1, N_HEADS, tm, n), lambda b, i: (b, 0, i, 0))
    in_specs = [row(d), vec(6 * d), _const_spec(g.shape),
                _const_spec(w_up.shape), _const_spec(w_down.shape)]
    args = [x, mod, g, w_up, w_down]
    out_specs = [row(d)]
    out_shape = [jax.ShapeDtypeStruct(x.shape, F32)]
    name = "mlp"
    if oproj is not None:
        attn, w_o = oproj
        in_specs += [heads(attn.shape[3]), _const_spec(w_o.shape)]
        args += [attn, w_o]
        name = "oproj_" + name
    if kv is not None:
        assert len(kv) == N_KV_IN
        kv_mod, kv_g, w_a, g_a, w_k, w_vt, cos_t, sin_t = kv
        n_v = w_vt.shape[0]
        in_specs += [vec(2 * d), _const_spec((1, d)), _const_spec(w_a.shape),
                     _const_spec((1, KV_LORA)), _const_spec(w_k.shape),
                     _const_spec(w_vt.shape), table(LANES), table(LANES)]
        args += [kv_mod, kv_g.reshape(1, d), w_a, g_a.reshape(1, KV_LORA), w_k,
                 w_vt, cos_t, sin_t]
        out_specs += [heads(QK_NOPE),
                      pl.BlockSpec((1, n_v, tm), lambda b, i: (b, 0, i)),
                      row(LANES)]
        out_shape += [jax.ShapeDtypeStruct((bsz, N_HEADS, s, QK_NOPE), BF16),
                      jax.ShapeDtypeStruct((bsz, n_v, s), BF16),
                      jax.ShapeDtypeStruct((bsz, s, LANES), BF16)]
        name += "_kv"
    if q is not None:
        assert len(q) == N_Q_IN
        q_mod, q_g, w_a, g_a, w_b, cos_t, sin_t = q
        in_specs += [vec(6 * d), _const_spec(q_g.shape), _const_spec(w_a.shape),
                     _const_spec((1, Q_LORA)), _const_spec(w_b.shape),
                     table(2 * LANES), table(2 * LANES)]
        args += [q_mod, q_g, w_a, g_a.reshape(1, Q_LORA), w_b, cos_t, sin_t]
        out_specs += [heads(2 * LANES)]
        out_shape += [jax.ShapeDtypeStruct((bsz, N_HEADS, s, 2 * LANES), BF16)]
        name += "_q"
    return pl.pallas_call(
        functools.partial(_mlp_kernel, with_oproj=oproj is not None,
                          with_kv=kv is not None, with_q=q is not None),
        out_shape=tuple(out_shape),
        grid=(bsz, s // tm),
        in_specs=in_specs,
        out_specs=tuple(out_specs),
        compiler_params=_params(2),
        name=name,
    )(*args)


def _col_reduce(x, op, final):
    n = x.shape[0]
    while n > 8 and n % 16 == 0:
        n //= 2
        x = op(x[:n], x[n:])
    return final(x, axis=0, keepdims=True)


def _attn_kernel(q_ref, kn_ref, kpe_ref, vt_ref, o_ref, kc_ref, s_ref, p_ref):
    s = q_ref.shape[2]
    tq = Q_TILE
    kc_ref[:, 0:LANES] = kn_ref[0, 0]
    kc_ref[:, LANES:2 * LANES] = kpe_ref[0]
    k_chunk = lax.broadcasted_iota(jnp.int32, (tq, tq), 0) // CHUNK
    q_chunk = lax.broadcasted_iota(jnp.int32, (tq, tq), 1) // CHUNK
    diag_mask = k_chunk <= q_chunk
    neg = jnp.finfo(F32).min
    n_tiles = s // tq

    def scores(i, slot):
        lo, hi = i * tq, (i + 1) * tq
        q = q_ref[0, 0, lo:hi, :]
        if i > 0:
            s_ref[slot, 0:lo, :] = _dot_nt(kc_ref[0:lo, :], q)
        s_ref[slot, lo:hi, :] = jnp.where(
            diag_mask, _dot_nt(kc_ref[lo:hi, :], q), neg)

    def softmax_pv(i, slot):
        lo, hi = i * tq, (i + 1) * tq
        blocks = [(j * tq, (j + 1) * tq) for j in range(i + 1)]
        m = None
        for a, b in blocks:
            mb = _col_reduce(s_ref[slot, a:b, :], jnp.maximum, jnp.max)
            m = mb if m is None else jnp.maximum(m, mb)
        l = None
        for a, b in blocks:
            p = jnp.exp2(s_ref[slot, a:b, :] - m)
            lb = _col_reduce(p, jnp.add, jnp.sum)
            l = lb if l is None else l + lb
            p_ref[slot, a:b, :] = p.astype(BF16)
        acc = _dot(vt_ref[0, :, 0:hi], p_ref[slot, 0:hi, :])
        o_ref[0, 0, lo:hi, :] = (acc * (1.0 / l)).T.astype(BF16)

    order = list(range(1, n_tiles, 2)) + list(range(n_tiles - 2 + n_tiles % 2, -1, -2))
    scores(order[0], 0)
    for pos, i in enumerate(order):
        if pos + 1 < n_tiles:
            scores(order[pos + 1], (pos + 1) % 2)
        softmax_pv(i, pos % 2)


def _attention(q_cat, k_nope, k_pe, v_t):
    bsz, _, s, _ = q_cat.shape
    head = lambda n: pl.BlockSpec((1, 1, s, n), lambda b, h: (b, h, 0, 0))
    return pl.pallas_call(
        _attn_kernel,
        out_shape=jax.ShapeDtypeStruct((bsz, N_HEADS, s, V_HEAD), BF16),
        grid=(bsz, N_HEADS),
        in_specs=[
            head(2 * LANES),
            head(QK_NOPE),
            pl.BlockSpec((1, s, LANES), lambda b, h: (b, 0, 0)),
            pl.BlockSpec((1, V_HEAD, s), lambda b, h: (b, h, 0)),
        ],
        out_specs=head(V_HEAD),
        scratch_shapes=[pltpu.VMEM((s, 2 * LANES), BF16),
                        pltpu.VMEM((2, s, Q_TILE), F32),
                        pltpu.VMEM((2, s, Q_TILE), BF16)],
        compiler_params=_params(2),
        name="mla_attention",
    )(q_cat, k_nope, k_pe, v_t)


def _half_swap(w):
    half = w.shape[-1] // 2
    return jnp.concatenate([w[..., half:], w[..., :half]], axis=-1)


def _rope_tables(seq_len):
    inv_freq = 1.0 / (ROPE_THETA ** (jnp.arange(0, QK_ROPE, 2, dtype=F32) / QK_ROPE))
    ang = jnp.arange(seq_len, dtype=F32)[:, None] * inv_freq[None, :]
    cos, sin = jnp.cos(ang), jnp.sin(ang)
    cos2 = jnp.concatenate([cos, cos], axis=-1)
    sin2 = jnp.concatenate([-sin, sin], axis=-1)
    zeros = jnp.zeros((seq_len, QK_ROPE), F32)
    k_cos = jnp.concatenate([cos2, zeros], axis=-1)
    k_sin = jnp.concatenate([sin2, zeros], axis=-1)
    sm_scale = math.log2(math.e) / math.sqrt(QK_NOPE + QK_ROPE)
    q_cos = jnp.concatenate([jnp.full((seq_len, QK_NOPE), sm_scale, F32),
                             cos2 * sm_scale, zeros], axis=-1)
    q_sin = jnp.concatenate([jnp.zeros((seq_len, QK_NOPE), F32),
                             sin2 * sm_scale, zeros], axis=-1)
    return k_cos, k_sin, q_cos, q_sin


def kernel(x, c, ada_w, ada_b, norm_g, conv_in_w, conv_w, conv_b, conv_out_w,
           kv_ada_w, kv_ada_b, kv_norm_g, kv_a_w, kv_a_norm_g, kv_b_w,
           q_a_w, q_a_norm_g, q_b_w, attn_o_w, mlp_up_w, mlp_down_w):
    bsz, s, d = x.shape
    depth = ada_w.shape[0]
    n_a = conv_in_w.shape[0]
    assert 1 <= n_a < depth, "needs a conv layer before the first attention layer"

    mod = _ada(c, ada_w, ada_b).reshape(depth, bsz, 1, 6 * d)
    kv_mod = _ada(c, kv_ada_w[None], kv_ada_b[None]).reshape(bsz, 1, 2 * d)
    k_cos, k_sin, q_cos, q_sin = _rope_tables(s)

    w_pe = kv_a_w[:, KV_LORA:]
    w_kva = jnp.concatenate([kv_a_w[:, :KV_LORA], w_pe, _half_swap(w_pe)], axis=1)
    w_kvb = kv_b_w.reshape(KV_LORA, N_HEADS, QK_NOPE + V_HEAD)
    w_k = w_kvb[:, :, :QK_NOPE].reshape(KV_LORA, -1)
    w_vt = w_kvb[:, :, QK_NOPE:].reshape(KV_LORA, -1).T
    kv_args = (kv_mod, kv_norm_g, w_kva.astype(BF16), kv_a_norm_g,
               w_k.astype(BF16), w_vt.astype(BF16), k_cos, k_sin)

    def q_args(l):
        j = l - n_a
        w_qb = q_b_w[j].reshape(Q_LORA, N_HEADS, QK_NOPE + QK_ROPE)
        w_qb = jnp.concatenate([w_qb, _half_swap(w_qb[:, :, QK_NOPE:])], axis=-1)
        w_qb = w_qb.reshape(Q_LORA, N_HEADS * 2 * LANES)
        return (mod[l], norm_g[l], q_a_w[j].astype(BF16), q_a_norm_g[j],
                w_qb.astype(BF16), q_cos, q_sin)

    k_nope = v_t = k_pe = q_cat = None
    for l in range(depth):
        oproj = None
        if l < n_a:
            x = _conv_layer(x, mod[l], norm_g[l], conv_in_w[l].astype(BF16),
                            conv_w[l], conv_b[l], conv_out_w[l].astype(BF16))
        else:
            attn = _attention(q_cat, k_nope, k_pe, v_t)
            oproj = (attn, attn_o_w[l - n_a].astype(BF16))
        kv = kv_args if l + 1 == n_a else None
        q = q_args(l + 1) if n_a <= l + 1 < depth else None
        outs = _mlp_layer(x, mod[l], norm_g[l], mlp_up_w[l].astype(BF16),
                          mlp_down_w[l].astype(BF16), oproj, kv, q)
        x = outs[0]
        if kv is not None:
            k_nope, v_t, k_pe = outs[1:4]
        if q is not None:
            q_cat = outs[-1]
    return x
```

```python
import functools
import math

import jax
import jax.numpy as jnp
from jax import lax
from jax.experimental import pallas as pl
from jax.experimental.pallas import tpu as pltpu

CHUNK = 64
CONV_W = 3
N_HEADS = 8
QK_NOPE = 128
QK_ROPE = 64
V_HEAD = 128
Q_LORA = 384
KV_LORA = 256
ROPE_THETA = 10000.0
NORM_EPS = 1e-6

V7X_VMEM_LIMIT_BYTES = 56 * 1024 * 1024
LANES = 128

TOKEN_TILE = 512
FF_CHUNK = 1024
ADA_COL_TILE = 1536
Q_TILE = 256
ROW_SUB = 256
HEADS_PER_STEP = 4

F32 = jnp.float32
BF16 = jnp.bfloat16

Q_SCALE = math.log2(math.e) / math.sqrt(QK_NOPE + QK_ROPE)


def _normalize(x):
    ms = jnp.mean(x * x, axis=-1, keepdims=True)
    return x * lax.rsqrt(ms + NORM_EPS)


def _rms(x, g):
    return _normalize(x) * g


def _mod_gain(g, scale):
    return g * (1.0 + scale)


def _dot(a, b):
    return jnp.dot(a, b, preferred_element_type=F32)


def _dot_nt(a, b):
    return lax.dot_general(a, b, (((1,), (1,)), ((), ())),
                           preferred_element_type=F32)


def _const_spec(shape):
    nd = len(shape)
    return pl.BlockSpec(shape, lambda *_: (0,) * nd,
                        pipeline_mode=pl.Buffered(1))


def _params(n_axes):
    return pltpu.CompilerParams(
        dimension_semantics=("arbitrary",) * n_axes,
        vmem_limit_bytes=V7X_VMEM_LIMIT_BYTES)


def _ada_kernel(c_ref, w_ref, b_ref, o_ref):
    c = c_ref[...]
    c_act = (c * jax.nn.sigmoid(c)).astype(BF16)
    o_ref[0] = _dot(c_act, w_ref[0].astype(BF16)) + b_ref[0]


def _ada(c, w, b):
    n_layers, d, n = w.shape
    bsz = c.shape[0]
    tn = max(t for t in range(LANES, ADA_COL_TILE + 1, LANES) if n % t == 0)
    return pl.pallas_call(
        _ada_kernel,
        out_shape=jax.ShapeDtypeStruct((n_layers, bsz, n), F32),
        grid=(n_layers, n // tn),
        in_specs=[
            _const_spec((bsz, d)),
            pl.BlockSpec((1, d, tn), lambda l, j: (l, 0, j)),
            pl.BlockSpec((1, 1, tn), lambda l, j: (l, 0, j)),
        ],
        out_specs=pl.BlockSpec((1, bsz, tn), lambda l, j: (l, 0, j)),
        compiler_params=_params(2),
        name="ada_mod",
    )(c, w, b.reshape(n_layers, 1, n))


def _conv_kernel(x_ref, mod_ref, g_ref, win_ref, cw_ref, cb_ref, wout_ref,
                 o_ref, u_ref):
    tm, d = x_ref.shape[1], x_ref.shape[2]
    rs = ROW_SUB
    mod = mod_ref[0]
    shift, gate = mod[:, 0:d], mod[:, 2 * d:3 * d]
    gain = _mod_gain(g_ref[0:1, :], mod[:, d:2 * d])

    @pl.when(pl.program_id(1) == 0)
    def _():
        u_ref[0:8, :] = jnp.zeros((8, d), F32)

    @pl.when(pl.program_id(1) > 0)
    def _():
        u_ref[0:8, :] = u_ref[tm:tm + 8, :]

    def project(r):
        x = x_ref[0, r * rs:(r + 1) * rs, :]
        h = _normalize(x) * gain + shift
        return _dot(h.astype(BF16), win_ref[...])

    def mix(r, bcx):
        lo = 8 + r * rs
        u = bcx[:, d:2 * d] * bcx[:, 2 * d:3 * d]
        u_ref[lo:lo + rs, :] = u
        conv = (cw_ref[0:1, :] * u_ref[lo - 2:lo - 2 + rs, :]
                + cw_ref[1:2, :] * u_ref[lo - 1:lo - 1 + rs, :]
                + cw_ref[2:3, :] * u
                + cb_ref[...])
        y = _dot((bcx[:, 0:d] * conv).astype(BF16), wout_ref[...])
        rows = slice(r * rs, (r + 1) * rs)
        o_ref[0, rows, :] = x_ref[0, rows, :] + gate * _rms(y, g_ref[1:2, :])

    nxt = project(0)
    for r in range(tm // rs):
        cur = nxt
        if r + 1 < tm // rs:
            nxt = project(r + 1)
        mix(r, cur)


def _conv_layer(x, mod, g, w_in, conv_w, conv_b, w_out):
    bsz, s, d = x.shape
    tm = TOKEN_TILE
    return pl.pallas_call(
        _conv_kernel,
        out_shape=jax.ShapeDtypeStruct(x.shape, F32),
        grid=(bsz, s // tm),
        in_specs=[
            pl.BlockSpec((1, tm, d), lambda b, i: (b, i, 0)),
            pl.BlockSpec((1, 1, 6 * d), lambda b, i: (b, 0, 0)),
            _const_spec(g.shape),
            _const_spec(w_in.shape),
            _const_spec(conv_w.shape),
            _const_spec((1, d)),
            _const_spec(w_out.shape),
        ],
        out_specs=pl.BlockSpec((1, tm, d), lambda b, i: (b, i, 0)),
        scratch_shapes=[pltpu.VMEM((tm + 8, d), F32)],
        compiler_params=_params(2),
        name="conv_mixer",
    )(x, mod, g, w_in, conv_w, conv_b.reshape(1, d), w_out)


def _kv_tail(xn, rows, mod_ref, g_ref, wa_ref, ga_ref, wk_ref, wvt_ref, tab_ref,
             kn_ref, vt_ref, kpe_ref):
    d = xn.shape[1]
    mod = mod_ref[0]
    h = xn * _mod_gain(g_ref[...], mod[:, d:2 * d]) + mod[:, 0:d]
    kva = _dot(h.astype(BF16), wa_ref[...])
    c_kv = _rms(kva[:, 0:KV_LORA], ga_ref[...]).astype(BF16)
    pe = kva[:, KV_LORA:KV_LORA + LANES]
    pe_sw = pltpu.roll(pe, QK_ROPE, 1)
    tab = tab_ref[rows, :]
    kpe_ref[0, 0, rows, :] = (pe * tab[:, 0:LANES]
                              + pe_sw * tab[:, LANES:2 * LANES]).astype(BF16)
    kpe_ref[0, 1, rows, :] = (pe_sw * tab[:, 2 * LANES:3 * LANES]
                              + pe * tab[:, 3 * LANES:4 * LANES]).astype(BF16)
    kn = _dot(c_kv, wk_ref[...]).astype(BF16)
    for hd in range(N_HEADS):
        kn_ref[0, hd, rows, :] = kn[:, hd * QK_NOPE:(hd + 1) * QK_NOPE]
    vt_ref[0, :, rows] = _dot_nt(wvt_ref[...], c_kv).astype(BF16)


def _q_tail(xn, rows, mod_ref, g_ref, wa_ref, ga_ref, wb_ref, tab_ref, q_ref):
    d = xn.shape[1]
    mod = mod_ref[0]
    h = xn * _mod_gain(g_ref[0:1, :], mod[:, d:2 * d]) + mod[:, 0:d]
    qa = _rms(_dot(h.astype(BF16), wa_ref[...]), ga_ref[...])
    q = _dot(qa.astype(BF16), wb_ref[...])
    tab = tab_ref[rows, :]
    pw = 4 * LANES
    for pair in range(N_HEADS // 2):
        base = pair * pw
        x_pe = q[:, base + 2 * LANES:base + 3 * LANES]
        x_sw = q[:, base + 3 * LANES:base + 4 * LANES]
        for k in range(2):
            nope = q[:, base + k * LANES:base + (k + 1) * LANES] * Q_SCALE
            rope = (x_pe * tab[:, 2 * k * LANES:(2 * k + 1) * LANES]
                    + x_sw * tab[:, (2 * k + 1) * LANES:(2 * k + 2) * LANES])
            q_ref[0, 2 * pair + k, rows, 0:LANES] = nope.astype(BF16)
            q_ref[0, 2 * pair + k, rows, LANES:2 * LANES] = rope.astype(BF16)


N_KV_IN, N_KV_OUT = 7, 3
N_Q_IN, N_Q_OUT = 6, 1


def _mlp_kernel(*refs, with_oproj, with_kv, with_q):
    refs = list(refs)
    x_ref, mod_ref, g_ref, up_ref, down_ref = refs[:5]
    pos = 5
    if with_oproj:
        a_ref, ow_ref = refs[pos:pos + 2]
        pos += 2
    if with_kv:
        kv_in = refs[pos:pos + N_KV_IN]
        pos += N_KV_IN
    if with_q:
        q_in = refs[pos:pos + N_Q_IN]
        pos += N_Q_IN
    o_ref = refs[pos]
    pos += 1
    if with_kv:
        kv_out = refs[pos:pos + N_KV_OUT]
        pos += N_KV_OUT
    if with_q:
        q_out = refs[pos:pos + N_Q_OUT]

    tm, d = x_ref.shape[1], x_ref.shape[2]
    d_ff = up_ref.shape[1]
    rs = ROW_SUB
    n_sub = tm // rs
    mod = mod_ref[0]
    gate_m = mod[:, 2 * d:3 * d]
    shift, gate = mod[:, 3 * d:4 * d], mod[:, 5 * d:6 * d]
    gain = _mod_gain(g_ref[2:3, :], mod[:, 4 * d:5 * d])
    chunks = list(range(0, d_ff, FF_CHUNK))

    def prologue(r):
        rows = slice(r * rs, (r + 1) * rs)
        x = x_ref[0, rows, :]
        if with_oproj:
            a = jnp.concatenate([a_ref[0, hd, rows, :] for hd in range(N_HEADS)],
                                axis=1)
            y = _dot(a, ow_ref[...])
            x = x + gate_m * _rms(y, g_ref[1:2, :])
        h = (_normalize(x) * gain + shift).astype(BF16)
        return x, h

    def ff_chunk(h, k, y):
        hid = _dot(h, up_ref[:, k:k + FF_CHUNK])
        act = jnp.square(jnp.maximum(hid, 0.0)).astype(BF16)
        part = _dot(act, down_ref[k:k + FF_CHUNK, :])
        return part if y is None else y + part

    def epilogue(r, x, y):
        rows = slice(r * rs, (r + 1) * rs)
        x_new = x + gate * _rms(y, g_ref[3:4, :])
        o_ref[0, rows, :] = x_new
        if with_kv or with_q:
            xn = _normalize(x_new)
        if with_kv:
            _kv_tail(xn, rows, *kv_in, *kv_out)
        if with_q:
            _q_tail(xn, rows, *q_in, *q_out)

    x, h = prologue(0)
    pending = None
    for r in range(n_sub):
        y = None
        for ci, k in enumerate(chunks):
            if ci == len(chunks) - 1 and r + 1 < n_sub:
                nxt = prologue(r + 1)
            y = ff_chunk(h, k, y)
            if ci == 0 and pending is not None:
                epilogue(*pending)
                pending = None
        pending = (r, x, y)
        if r + 1 < n_sub:
            x, h = nxt
    epilogue(*pending)


def _mlp_layer(x, mod, g, w_up, w_down, oproj=None, kv=None, q=None):
    bsz, s, d = x.shape
    tm = TOKEN_TILE
    row = lambda n: pl.BlockSpec((1, tm, n), lambda b, i: (b, i, 0))
    vec = lambda n: pl.BlockSpec((1, 1, n), lambda b, i: (b, 0, 0))
    table = lambda n: pl.BlockSpec((tm, n), lambda b, i: (i, 0))
    heads = lambda n: pl.BlockSpec((1, N_HEADS, tm, n), lambda b, i: (b, 0, i, 0))
    in_specs = [row(d), vec(6 * d), _const_spec(g.shape),
                _const_spec(w_up.shape), _const_spec(w_down.shape)]
    args = [x, mod, g, w_up, w_down]
    out_specs = [row(d)]
    out_shape = [jax.ShapeDtypeStruct(x.shape, F32)]
    name = "mlp"
    if oproj is not None:
        attn, w_o = oproj
        in_specs += [heads(attn.shape[3]), _const_spec(w_o.shape)]
        args += [attn, w_o]
        name = "oproj_" + name
    if kv is not None:
        assert len(kv) == N_KV_IN
        kv_mod, kv_g, w_a, g_a, w_k, w_vt, rope_tab = kv
        n_v = w_vt.shape[0]
        in_specs += [vec(2 * d), _const_spec((1, d)), _const_spec(w_a.shape),
                     _const_spec((1, KV_LORA)), _const_spec(w_k.shape),
                     _const_spec(w_vt.shape), table(4 * LANES)]
        args += [kv_mod, kv_g.reshape(1, d), w_a, g_a.reshape(1, KV_LORA), w_k,
                 w_vt, rope_tab]
        out_specs += [heads(QK_NOPE),
                      pl.BlockSpec((1, n_v, tm), lambda b, i: (b, 0, i)),
                      pl.BlockSpec((1, 2, tm, LANES), lambda b, i: (b, 0, i, 0))]
        out_shape += [jax.ShapeDtypeStruct((bsz, N_HEADS, s, QK_NOPE), BF16),
                      jax.ShapeDtypeStruct((bsz, n_v, s), BF16),
                      jax.ShapeDtypeStruct((bsz, 2, s, LANES), BF16)]
        name += "_kv"
    if q is not None:
        assert len(q) == N_Q_IN
        q_mod, q_g, w_a, g_a, w_b, rope_tab = q
        in_specs += [vec(6 * d), _const_spec(q_g.shape), _const_spec(w_a.shape),
                     _const_spec((1, Q_LORA)), _const_spec(w_b.shape),
                     table(4 * LANES)]
        args += [q_mod, q_g, w_a, g_a.reshape(1, Q_LORA), w_b, rope_tab]
        out_specs += [heads(2 * LANES)]
        out_shape += [jax.ShapeDtypeStruct((bsz, N_HEADS, s, 2 * LANES), BF16)]
        name += "_q"
    return pl.pallas_call(
        functools.partial(_mlp_kernel, with_oproj=oproj is not None,
                          with_kv=kv is not None, with_q=q is not None),
        out_shape=tuple(out_shape),
        grid=(bsz, s // tm),
        in_specs=in_specs,
        out_specs=tuple(out_specs),
        compiler_params=_params(2),
        name=name,
    )(*args)


def _col_reduce(x, op, final):
    n = x.shape[0]
    while n > 8 and n % 16 == 0:
        n //= 2
        x = op(x[:n], x[n:])
    return final(x, axis=0, keepdims=True)


def _attn_kernel(q_ref, kn_ref, kpe_ref, vt_ref, o_ref, kc_ref, s_ref, p_ref):
    s = q_ref.shape[2]
    tq = Q_TILE
    for hh in range(HEADS_PER_STEP):
        kc_ref[hh, :, 0:LANES] = kn_ref[0, hh]
        kc_ref[hh, :, LANES:2 * LANES] = kpe_ref[0, hh % 2]
    k_chunk = lax.broadcasted_iota(jnp.int32, (tq, tq), 0) // CHUNK
    q_chunk = lax.broadcasted_iota(jnp.int32, (tq, tq), 1) // CHUNK
    diag_mask = k_chunk <= q_chunk
    neg = jnp.finfo(F32).min
    n_tiles = s // tq

    def scores(item, slot):
        hh, i = item
        lo, hi = i * tq, (i + 1) * tq
        q = q_ref[0, hh, lo:hi, :]
        s_d = jnp.where(diag_mask, _dot_nt(kc_ref[hh, lo:hi, :], q), neg)
        s_ref[slot, lo:hi, :] = s_d
        m = _col_reduce(s_d, jnp.maximum, jnp.max)
        for a in range(0, lo, tq):
            s_o = _dot_nt(kc_ref[hh, a:a + tq, :], q)
            s_ref[slot, a:a + tq, :] = s_o
            m = jnp.maximum(m, _col_reduce(s_o, jnp.maximum, jnp.max))
        return m

    def softmax_pv(item, slot, m):
        hh, i = item
        lo, hi = i * tq, (i + 1) * tq
        blocks = [(j * tq, (j + 1) * tq) for j in range(i + 1)]
        l = None
        for a, b in blocks:
            p = jnp.exp2(s_ref[slot, a:b, :] - m)
            lb = _col_reduce(p, jnp.add, jnp.sum)
            l = lb if l is None else l + lb
            p_ref[slot, a:b, :] = p.astype(BF16)
        acc = _dot(vt_ref[0, hh * V_HEAD:(hh + 1) * V_HEAD, 0:hi],
                   p_ref[slot, 0:hi, :])
        o_ref[0, hh, lo:hi, :] = (acc * (1.0 / l)).T.astype(BF16)

    order = list(range(1, n_tiles, 2)) + list(range(n_tiles - 2 + n_tiles % 2, -1, -2))
    items = [(hh, i) for hh in range(HEADS_PER_STEP) for i in order]
    m_next = scores(items[0], 0)
    for pos, item in enumerate(items):
        m = m_next
        if pos + 1 < len(items):
            m_next = scores(items[pos + 1], (pos + 1) % 2)
        softmax_pv(item, pos % 2, m)


def _attention(q_cat, k_nope, k_pe, v_t):
    bsz, _, s, _ = q_cat.shape
    hps = HEADS_PER_STEP
    assert hps % 2 == 0 and N_HEADS % hps == 0
    head = lambda n: pl.BlockSpec((1, hps, s, n), lambda b, h: (b, h, 0, 0))
    return pl.pallas_call(
        _attn_kernel,
        out_shape=jax.ShapeDtypeStruct((bsz, N_HEADS, s, V_HEAD), BF16),
        grid=(bsz, N_HEADS // hps),
        in_specs=[
            head(2 * LANES),
            head(QK_NOPE),
            pl.BlockSpec((1, 2, s, LANES), lambda b, h: (b, 0, 0, 0)),
            pl.BlockSpec((1, hps * V_HEAD, s), lambda b, h: (b, h, 0)),
        ],
        out_specs=head(V_HEAD),
        scratch_shapes=[pltpu.VMEM((hps, s, 2 * LANES), BF16),
                        pltpu.VMEM((2, s, Q_TILE), F32),
                        pltpu.VMEM((2, s, Q_TILE), BF16)],
        compiler_params=_params(2),
        name="mla_attention",
    )(q_cat, k_nope, k_pe, v_t)


def _half_swap(w):
    half = w.shape[-1] // 2
    return jnp.concatenate([w[..., half:], w[..., :half]], axis=-1)


def _rope_tables(seq_len):
    inv_freq = 1.0 / (ROPE_THETA ** (jnp.arange(0, QK_ROPE, 2, dtype=F32) / QK_ROPE))
    ang = jnp.arange(seq_len, dtype=F32)[:, None] * inv_freq[None, :]
    cos, sin = jnp.cos(ang), jnp.sin(ang)
    cos2 = jnp.concatenate([cos, cos], axis=-1)
    sin2 = jnp.concatenate([-sin, sin], axis=-1)
    zeros = jnp.zeros((seq_len, QK_ROPE), F32)
    k_tab = jnp.concatenate([cos2, zeros, sin2, zeros, zeros, cos2, zeros, sin2],
                            axis=-1)
    return k_tab, k_tab * Q_SCALE


def kernel(x, c, ada_w, ada_b, norm_g, conv_in_w, conv_w, conv_b, conv_out_w,
           kv_ada_w, kv_ada_b, kv_norm_g, kv_a_w, kv_a_norm_g, kv_b_w,
           q_a_w, q_a_norm_g, q_b_w, attn_o_w, mlp_up_w, mlp_down_w):
    bsz, s, d = x.shape
    depth = ada_w.shape[0]
    n_a = conv_in_w.shape[0]
    assert 1 <= n_a < depth, "needs a conv layer before the first attention layer"

    mod = _ada(c, ada_w, ada_b).reshape(depth, bsz, 1, 6 * d)
    kv_mod = _ada(c, kv_ada_w[None], kv_ada_b[None]).reshape(bsz, 1, 2 * d)
    k_tab, q_tab = _rope_tables(s)

    w_pe = kv_a_w[:, KV_LORA:]
    w_kva = jnp.concatenate([kv_a_w[:, :KV_LORA], w_pe, _half_swap(w_pe)], axis=1)
    w_kvb = kv_b_w.reshape(KV_LORA, N_HEADS, QK_NOPE + V_HEAD)
    w_k = w_kvb[:, :, :QK_NOPE].reshape(KV_LORA, -1)
    w_vt = w_kvb[:, :, QK_NOPE:].reshape(KV_LORA, -1).T
    kv_args = (kv_mod, kv_norm_g, w_kva.astype(BF16), kv_a_norm_g,
               w_k.astype(BF16), w_vt.astype(BF16), k_tab)

    def q_args(l):
        j = l - n_a
        w_qb = q_b_w[j].reshape(Q_LORA, N_HEADS // 2, 2, QK_NOPE + QK_ROPE)
        w_nope = w_qb[..., :QK_NOPE].reshape(Q_LORA, N_HEADS // 2, 2 * QK_NOPE)
        w_rope = w_qb[..., QK_NOPE:]
        w_swap = _half_swap(w_rope).reshape(Q_LORA, N_HEADS // 2, 2 * QK_ROPE)
        w_rope = w_rope.reshape(Q_LORA, N_HEADS // 2, 2 * QK_ROPE)
        w_qb = jnp.concatenate([w_nope, w_rope, w_swap], axis=-1)
        w_qb = w_qb.reshape(Q_LORA, N_HEADS * 2 * LANES)
        return (mod[l], norm_g[l], q_a_w[j].astype(BF16), q_a_norm_g[j],
                w_qb.astype(BF16), q_tab)

    k_nope = v_t = k_pe = q_cat = None
    for l in range(depth):
        oproj = None
        if l < n_a:
            x = _conv_layer(x, mod[l], norm_g[l], conv_in_w[l].astype(BF16),
                            conv_w[l], conv_b[l], conv_out_w[l].astype(BF16))
        else:
            attn = _attention(q_cat, k_nope, k_pe, v_t)
            oproj = (attn, attn_o_w[l - n_a].astype(BF16))
        kv = kv_args if l + 1 == n_a else None
        q = q_args(l + 1) if n_a <= l + 1 < depth else None
        outs = _mlp_layer(x, mod[l], norm_g[l], mlp_up_w[l].astype(BF16),
                          mlp_down_w[l].astype(BF16), oproj, kv, q)
        x = outs[0]
        if kv is not None:
            k_nope, v_t, k_pe = outs[1:4]
        if q is not None:
            q_cat = outs[-1]
    return x
```

```python
import functools
import math

import jax
import jax.numpy as jnp
from jax import lax
from jax.experimental import pallas as pl
from jax.experimental.pallas import tpu as pltpu

CHUNK = 64
CONV_W = 3
N_HEADS = 8
QK_NOPE = 128
QK_ROPE = 64
V_HEAD = 128
Q_LORA = 384
KV_LORA = 256
ROPE_THETA = 10000.0
NORM_EPS = 1e-6

V7X_VMEM_LIMIT_BYTES = 56 * 1024 * 1024
LANES = 128

TOKEN_TILE = 512
FF_CHUNK = 1024
ADA_COL_TILE = 1536
Q_TILE = 256
ROW_SUB = 256
HEADS_PER_STEP = 4

F32 = jnp.float32
BF16 = jnp.bfloat16

Q_SCALE = math.log2(math.e) / math.sqrt(QK_NOPE + QK_ROPE)


def _normalize(x):
    ms = jnp.mean(x * x, axis=-1, keepdims=True)
    return x * lax.rsqrt(ms + NORM_EPS)


def _rms(x, g):
    return _normalize(x) * g


def _mod_gain(g, scale):
    return g * (1.0 + scale)


def _dot(a, b):
    return jnp.dot(a, b, preferred_element_type=F32)


def _dot_nt(a, b):
    return lax.dot_general(a, b, (((1,), (1,)), ((), ())),
                           preferred_element_type=F32)


def _const_spec(shape):
    nd = len(shape)
    return pl.BlockSpec(shape, lambda *_: (0,) * nd,
                        pipeline_mode=pl.Buffered(1))


def _params(n_axes):
    return pltpu.CompilerParams(
        dimension_semantics=("arbitrary",) * n_axes,
        vmem_limit_bytes=V7X_VMEM_LIMIT_BYTES)


def _cast_jobs(weights, grid):
    n_b, n_i = grid
    args, in_specs, out_specs, out_shapes = [], [], [], []
    for w, layer in weights:
        _, rows, cols = w.shape
        rb = rows // (n_b * n_i)
        assert rb * n_b * n_i == rows and rb % 16 == 0, (w.shape, grid)
        args.append(w)
        in_specs.append(pl.BlockSpec(
            (1, rb, cols), lambda b, i, layer=layer: (layer, b * n_i + i, 0)))
        out_specs.append(pl.BlockSpec((rb, cols), lambda b, i: (b * n_i + i, 0)))
        out_shapes.append(jax.ShapeDtypeStruct((rows, cols), BF16))
    return args, in_specs, out_specs, out_shapes


def _run_cast_jobs(src_refs, dst_refs):
    for src, dst in zip(src_refs, dst_refs):
        dst[...] = src[0].astype(BF16)


def _ada_kernel(c_ref, w_ref, b_ref, o_ref):
    c = c_ref[...]
    c_act = (c * jax.nn.sigmoid(c)).astype(BF16)
    o_ref[0] = _dot(c_act, w_ref[0].astype(BF16)) + b_ref[0]


def _ada(c, w, b):
    n_layers, d, n = w.shape
    bsz = c.shape[0]
    tn = max(t for t in range(LANES, ADA_COL_TILE + 1, LANES) if n % t == 0)
    return pl.pallas_call(
        _ada_kernel,
        out_shape=jax.ShapeDtypeStruct((n_layers, bsz, n), F32),
        grid=(n_layers, n // tn),
        in_specs=[
            _const_spec((bsz, d)),
            pl.BlockSpec((1, d, tn), lambda l, j: (l, 0, j)),
            pl.BlockSpec((1, 1, tn), lambda l, j: (l, 0, j)),
        ],
        out_specs=pl.BlockSpec((1, bsz, tn), lambda l, j: (l, 0, j)),
        compiler_params=_params(2),
        name="ada_mod",
    )(c, w, b.reshape(n_layers, 1, n))


def _conv_kernel(*refs, n_casts):
    x_ref, mod_ref, g_ref, win_ref, cw_ref, cb_ref, wout_ref = refs[:7]
    cast_src = refs[7:7 + n_casts]
    o_ref = refs[7 + n_casts]
    cast_dst = refs[8 + n_casts:8 + 2 * n_casts]
    u_ref = refs[8 + 2 * n_casts]
    _run_cast_jobs(cast_src, cast_dst)
    tm, d = x_ref.shape[1], x_ref.shape[2]
    rs = ROW_SUB
    mod = mod_ref[0]
    shift, gate = mod[:, 0:d], mod[:, 2 * d:3 * d]
    gain = _mod_gain(g_ref[0:1, :], mod[:, d:2 * d])

    @pl.when(pl.program_id(1) == 0)
    def _():
        u_ref[0:8, :] = jnp.zeros((8, d), F32)

    @pl.when(pl.program_id(1) > 0)
    def _():
        u_ref[0:8, :] = u_ref[tm:tm + 8, :]

    def project(r):
        x = x_ref[0, r * rs:(r + 1) * rs, :]
        h = _normalize(x) * gain + shift
        return _dot(h.astype(BF16), win_ref[...])

    def mix(r, bcx):
        lo = 8 + r * rs
        u = bcx[:, d:2 * d] * bcx[:, 2 * d:3 * d]
        u_ref[lo:lo + rs, :] = u
        conv = (cw_ref[0:1, :] * u_ref[lo - 2:lo - 2 + rs, :]
                + cw_ref[1:2, :] * u_ref[lo - 1:lo - 1 + rs, :]
                + cw_ref[2:3, :] * u
                + cb_ref[...])
        y = _dot((bcx[:, 0:d] * conv).astype(BF16), wout_ref[...])
        rows = slice(r * rs, (r + 1) * rs)
        o_ref[0, rows, :] = x_ref[0, rows, :] + gate * _rms(y, g_ref[1:2, :])

    nxt = project(0)
    for r in range(tm // rs):
        cur = nxt
        if r + 1 < tm // rs:
            nxt = project(r + 1)
        mix(r, cur)


def _conv_layer(x, mod, g, w_in, conv_w, conv_b, w_out, casts=()):
    bsz, s, d = x.shape
    tm = TOKEN_TILE
    grid = (bsz, s // tm)
    c_args, c_in, c_out, c_shapes = _cast_jobs(casts, grid)
    return pl.pallas_call(
        functools.partial(_conv_kernel, n_casts=len(c_args)),
        out_shape=(jax.ShapeDtypeStruct(x.shape, F32), *c_shapes),
        grid=grid,
        in_specs=[
            pl.BlockSpec((1, tm, d), lambda b, i: (b, i, 0)),
            pl.BlockSpec((1, 1, 6 * d), lambda b, i: (b, 0, 0)),
            _const_spec(g.shape),
            _const_spec(w_in.shape),
            _const_spec(conv_w.shape),
            _const_spec((1, d)),
            _const_spec(w_out.shape),
            *c_in,
        ],
        out_specs=(pl.BlockSpec((1, tm, d), lambda b, i: (b, i, 0)), *c_out),
        scratch_shapes=[pltpu.VMEM((tm + 8, d), F32)],
        compiler_params=_params(2),
        name="conv_mixer",
    )(x, mod, g, w_in, conv_w, conv_b.reshape(1, d), w_out, *c_args)


def _kv_tail(xn, rows, mod_ref, g_ref, wa_ref, ga_ref, wk_ref, wvt_ref, tab_ref,
             kn_ref, vt_ref, kpe_ref):
    d = xn.shape[1]
    mod = mod_ref[0]
    h = xn * _mod_gain(g_ref[...], mod[:, d:2 * d]) + mod[:, 0:d]
    kva = _dot(h.astype(BF16), wa_ref[...])
    c_kv = _rms(kva[:, 0:KV_LORA], ga_ref[...]).astype(BF16)
    pe = kva[:, KV_LORA:KV_LORA + LANES]
    pe_sw = pltpu.roll(pe, QK_ROPE, 1)
    tab = tab_ref[rows, :]
    kpe_ref[0, 0, rows, :] = (pe * tab[:, 0:LANES]
                              + pe_sw * tab[:, LANES:2 * LANES]).astype(BF16)
    kpe_ref[0, 1, rows, :] = (pe_sw * tab[:, 2 * LANES:3 * LANES]
                              + pe * tab[:, 3 * LANES:4 * LANES]).astype(BF16)
    kn = _dot(c_kv, wk_ref[...]).astype(BF16)
    for hd in range(N_HEADS):
        kn_ref[0, hd, rows, :] = kn[:, hd * QK_NOPE:(hd + 1) * QK_NOPE]
    vt_ref[0, :, rows] = _dot_nt(wvt_ref[...], c_kv).astype(BF16)


def _q_tail(xn, rows, mod_ref, g_ref, wa_ref, ga_ref, wb_ref, tab_ref, q_ref):
    d = xn.shape[1]
    mod = mod_ref[0]
    h = xn * _mod_gain(g_ref[0:1, :], mod[:, d:2 * d]) + mod[:, 0:d]
    qa = _rms(_dot(h.astype(BF16), wa_ref[...]), ga_ref[...])
    q = _dot(qa.astype(BF16), wb_ref[...])
    tab = tab_ref[rows, :]
    pw = 4 * LANES
    for pair in range(N_HEADS // 2):
        base = pair * pw
        x_pe = q[:, base + 2 * LANES:base + 3 * LANES]
        x_sw = q[:, base + 3 * LANES:base + 4 * LANES]
        for k in range(2):
            nope = q[:, base + k * LANES:base + (k + 1) * LANES] * Q_SCALE
            rope = (x_pe * tab[:, 2 * k * LANES:(2 * k + 1) * LANES]
                    + x_sw * tab[:, (2 * k + 1) * LANES:(2 * k + 2) * LANES])
            q_ref[0, 2 * pair + k, rows, 0:LANES] = nope.astype(BF16)
            q_ref[0, 2 * pair + k, rows, LANES:2 * LANES] = rope.astype(BF16)


N_KV_IN, N_KV_OUT = 7, 3
N_Q_IN, N_Q_OUT = 6, 1


def _mlp_kernel(*refs, with_oproj, with_kv, with_q, n_casts):
    refs = list(refs)
    x_ref, mod_ref, g_ref, up_ref, down_ref = refs[:5]
    pos = 5
    if with_oproj:
        a_ref, ow_ref = refs[pos:pos + 2]
        pos += 2
    if with_kv:
        kv_in = refs[pos:pos + N_KV_IN]
        pos += N_KV_IN
    if with_q:
        q_in = refs[pos:pos + N_Q_IN]
        pos += N_Q_IN
    cast_src = refs[pos:pos + n_casts]
    pos += n_casts
    o_ref = refs[pos]
    pos += 1
    if with_kv:
        kv_out = refs[pos:pos + N_KV_OUT]
        pos += N_KV_OUT
    if with_q:
        q_out = refs[pos:pos + N_Q_OUT]
        pos += N_Q_OUT
    cast_dst = refs[pos:pos + n_casts]
    _run_cast_jobs(cast_src, cast_dst)

    tm, d = x_ref.shape[1], x_ref.shape[2]
    d_ff = up_ref.shape[1]
    rs = ROW_SUB
    n_sub = tm // rs
    mod = mod_ref[0]
    gate_m = mod[:, 2 * d:3 * d]
    shift, gate = mod[:, 3 * d:4 * d], mod[:, 5 * d:6 * d]
    gain = _mod_gain(g_ref[2:3, :], mod[:, 4 * d:5 * d])
    chunks = list(range(0, d_ff, FF_CHUNK))

    def prologue(r):
        rows = slice(r * rs, (r + 1) * rs)
        x = x_ref[0, rows, :]
        if with_oproj:
            a = jnp.concatenate([a_ref[0, hd, rows, :] for hd in range(N_HEADS)],
                                axis=1)
            y = _dot(a, ow_ref[...])
            x = x + gate_m * _rms(y, g_ref[1:2, :])
        h = (_normalize(x) * gain + shift).astype(BF16)
        return x, h

    def ff_chunk(h, k, y):
        hid = _dot(h, up_ref[:, k:k + FF_CHUNK])
        act = jnp.square(jnp.maximum(hid, 0.0)).astype(BF16)
        part = _dot(act, down_ref[k:k + FF_CHUNK, :])
        return part if y is None else y + part

    def epilogue(r, x, y):
        rows = slice(r * rs, (r + 1) * rs)
        x_new = x + gate * _rms(y, g_ref[3:4, :])
        o_ref[0, rows, :] = x_new
        if with_kv or with_q:
            xn = _normalize(x_new)
        if with_kv:
            _kv_tail(xn, rows, *kv_in, *kv_out)
        if with_q:
            _q_tail(xn, rows, *q_in, *q_out)

    x, h = prologue(0)
    pending = None
    for r in range(n_sub):
        y = None
        for ci, k in enumerate(chunks):
            if ci == len(chunks) - 1 and r + 1 < n_sub:
                nxt = prologue(r + 1)
            y = ff_chunk(h, k, y)
            if ci == 0 and pending is not None:
                epilogue(*pending)
                pending = None
        pending = (r, x, y)
        if r + 1 < n_sub:
            x, h = nxt
    epilogue(*pending)


def _mlp_layer(x, mod, g, w_up, w_down, oproj=None, kv=None, q=None, casts=()):
    bsz, s, d = x.shape
    tm = TOKEN_TILE
    grid = (bsz, s // tm)
    c_args, c_in, c_out, c_shapes = _cast_jobs(casts, grid)
    row = lambda n: pl.BlockSpec((1, tm, n), lambda b, i: (b, i, 0))
    vec = lambda n: pl.BlockSpec((1, 1, n), lambda b, i: (b, 0, 0))
    table = lambda n: pl.BlockSpec((tm, n), lambda b, i: (i, 0))
    heads = lambda n: pl.BlockSpec((1, N_HEADS, tm, n), lambda b, i: (b, 0, i, 0))
    in_specs = [row(d), vec(6 * d), _const_spec(g.shape),
                _const_spec(w_up.shape), _const_spec(w_down.shape)]
    args = [x, mod, g, w_up, w_down]
    out_specs = [row(d)]
    out_shape = [jax.ShapeDtypeStruct(x.shape, F32)]
    name = "mlp"
    if oproj is not None:
        attn, w_o = oproj
        in_specs += [heads(attn.shape[3]), _const_spec(w_o.shape)]
        args += [attn, w_o]
        name = "oproj_" + name
    if kv is not None:
        assert len(kv) == N_KV_IN
        kv_mod, kv_g, w_a, g_a, w_k, w_vt, rope_tab = kv
        n_v = w_vt.shape[0]
        in_specs += [vec(2 * d), _const_spec((1, d)), _const_spec(w_a.shape),
                     _const_spec((1, KV_LORA)), _const_spec(w_k.shape),
                     _const_spec(w_vt.shape), table(4 * LANES)]
        args += [kv_mod, kv_g.reshape(1, d), w_a, g_a.reshape(1, KV_LORA), w_k,
                 w_vt, rope_tab]
        out_specs += [heads(QK_NOPE),
                      pl.BlockSpec((1, n_v, tm), lambda b, i: (b, 0, i)),
                      pl.BlockSpec((1, 2, tm, LANES), lambda b, i: (b, 0, i, 0))]
        out_shape += [jax.ShapeDtypeStruct((bsz, N_HEADS, s, QK_NOPE), BF16),
                      jax.ShapeDtypeStruct((bsz, n_v, s), BF16),
                      jax.ShapeDtypeStruct((bsz, 2, s, LANES), BF16)]
        name += "_kv"
    if q is not None:
        assert len(q) == N_Q_IN
        q_mod, q_g, w_a, g_a, w_b, rope_tab = q
        in_specs += [vec(6 * d), _const_spec(q_g.shape), _const_spec(w_a.shape),
                     _const_spec((1, Q_LORA)), _const_spec(w_b.shape),
                     table(4 * LANES)]
        args += [q_mod, q_g, w_a, g_a.reshape(1, Q_LORA), w_b, rope_tab]
        out_specs += [heads(2 * LANES)]
        out_shape += [jax.ShapeDtypeStruct((bsz, N_HEADS, s, 2 * LANES), BF16)]
        name += "_q"
    return pl.pallas_call(
        functools.partial(_mlp_kernel, with_oproj=oproj is not None,
                          with_kv=kv is not None, with_q=q is not None,
                          n_casts=len(c_args)),
        out_shape=(*out_shape, *c_shapes),
        grid=grid,
        in_specs=in_specs + c_in,
        out_specs=(*out_specs, *c_out),
        compiler_params=_params(2),
        name=name,
    )(*args, *c_args)


def _col_reduce(x, op, final):
    n = x.shape[0]
    while n > 8 and n % 16 == 0:
        n //= 2
        x = op(x[:n], x[n:])
    return final(x, axis=0, keepdims=True)


def _attn_kernel(q_ref, kn_ref, kpe_ref, vt_ref, o_ref, kc_ref, s_ref, p_ref):
    s = q_ref.shape[2]
    tq = Q_TILE
    for hh in range(HEADS_PER_STEP):
        kc_ref[hh, :, 0:LANES] = kn_ref[0, hh]
        kc_ref[hh, :, LANES:2 * LANES] = kpe_ref[0, hh % 2]
    k_chunk = lax.broadcasted_iota(jnp.int32, (tq, tq), 0) // CHUNK
    q_chunk = lax.broadcasted_iota(jnp.int32, (tq, tq), 1) // CHUNK
    diag_mask = k_chunk <= q_chunk
    neg = jnp.finfo(F32).min
    n_tiles = s // tq

    def scores(item, slot):
        hh, i = item
        lo, hi = i * tq, (i + 1) * tq
        q = q_ref[0, hh, lo:hi, :]
        s_d = jnp.where(diag_mask, _dot_nt(kc_ref[hh, lo:hi, :], q), neg)
        s_ref[slot, lo:hi, :] = s_d
        m = _col_reduce(s_d, jnp.maximum, jnp.max)
        for a in range(0, lo, tq):
            s_o = _dot_nt(kc_ref[hh, a:a + tq, :], q)
            s_ref[slot, a:a + tq, :] = s_o
            m = jnp.maximum(m, _col_reduce(s_o, jnp.maximum, jnp.max))
        return m

    def softmax_pv(item, slot, m):
        hh, i = item
        lo, hi = i * tq, (i + 1) * tq
        blocks = [(j * tq, (j + 1) * tq) for j in range(i + 1)]
        l = None
        for a, b in blocks:
            p = jnp.exp2(s_ref[slot, a:b, :] - m)
            lb = _col_reduce(p, jnp.add, jnp.sum)
            l = lb if l is None else l + lb
            p_ref[slot, a:b, :] = p.astype(BF16)
        acc = _dot(vt_ref[0, hh * V_HEAD:(hh + 1) * V_HEAD, 0:hi],
                   p_ref[slot, 0:hi, :])
        o_ref[0, hh, lo:hi, :] = (acc * (1.0 / l)).T.astype(BF16)

    order = list(range(1, n_tiles, 2)) + list(range(n_tiles - 2 + n_tiles % 2, -1, -2))
    items = [(hh, i) for hh in range(HEADS_PER_STEP) for i in order]
    m_next = scores(items[0], 0)
    for pos, item in enumerate(items):
        m = m_next
        if pos + 1 < len(items):
            m_next = scores(items[pos + 1], (pos + 1) % 2)
        softmax_pv(item, pos % 2, m)


def _attention(q_cat, k_nope, k_pe, v_t):
    bsz, _, s, _ = q_cat.shape
    hps = HEADS_PER_STEP
    assert hps % 2 == 0 and N_HEADS % hps == 0
    head = lambda n: pl.BlockSpec((1, hps, s, n), lambda b, h: (b, h, 0, 0))
    return pl.pallas_call(
        _attn_kernel,
        out_shape=jax.ShapeDtypeStruct((bsz, N_HEADS, s, V_HEAD), BF16),
        grid=(bsz, N_HEADS // hps),
        in_specs=[
            head(2 * LANES),
            head(QK_NOPE),
            pl.BlockSpec((1, 2, s, LANES), lambda b, h: (b, 0, 0, 0)),
            pl.BlockSpec((1, hps * V_HEAD, s), lambda b, h: (b, h, 0)),
        ],
        out_specs=head(V_HEAD),
        scratch_shapes=[pltpu.VMEM((hps, s, 2 * LANES), BF16),
                        pltpu.VMEM((2, s, Q_TILE), F32),
                        pltpu.VMEM((2, s, Q_TILE), BF16)],
        compiler_params=_params(2),
        name="mla_attention",
    )(q_cat, k_nope, k_pe, v_t)


def _half_swap(w):
    half = w.shape[-1] // 2
    return jnp.concatenate([w[..., half:], w[..., :half]], axis=-1)


def _rope_tables(seq_len):
    inv_freq = 1.0 / (ROPE_THETA ** (jnp.arange(0, QK_ROPE, 2, dtype=F32) / QK_ROPE))
    ang = jnp.arange(seq_len, dtype=F32)[:, None] * inv_freq[None, :]
    cos, sin = jnp.cos(ang), jnp.sin(ang)
    cos2 = jnp.concatenate([cos, cos], axis=-1)
    sin2 = jnp.concatenate([-sin, sin], axis=-1)
    zeros = jnp.zeros((seq_len, QK_ROPE), F32)
    k_tab = jnp.concatenate([cos2, zeros, sin2, zeros, zeros, cos2, zeros, sin2],
                            axis=-1)
    return k_tab, k_tab * Q_SCALE


def kernel(x, c, ada_w, ada_b, norm_g, conv_in_w, conv_w, conv_b, conv_out_w,
           kv_ada_w, kv_ada_b, kv_norm_g, kv_a_w, kv_a_norm_g, kv_b_w,
           q_a_w, q_a_norm_g, q_b_w, attn_o_w, mlp_up_w, mlp_down_w):
    bsz, s, d = x.shape
    depth = ada_w.shape[0]
    n_a = conv_in_w.shape[0]
    assert 1 <= n_a < depth, "needs a conv layer before the first attention layer"

    mod = _ada(c, ada_w, ada_b).reshape(depth, bsz, 1, 6 * d)
    kv_mod = _ada(c, kv_ada_w[None], kv_ada_b[None]).reshape(bsz, 1, 2 * d)
    k_tab, q_tab = _rope_tables(s)

    w_pe = kv_a_w[:, KV_LORA:]
    w_kva = jnp.concatenate([kv_a_w[:, :KV_LORA], w_pe, _half_swap(w_pe)], axis=1)
    w_kvb = kv_b_w.reshape(KV_LORA, N_HEADS, QK_NOPE + V_HEAD)
    w_k = w_kvb[:, :, :QK_NOPE].reshape(KV_LORA, -1)
    w_vt = w_kvb[:, :, QK_NOPE:].reshape(KV_LORA, -1).T
    kv_args = (kv_mod, kv_norm_g, w_kva.astype(BF16), kv_a_norm_g,
               w_k.astype(BF16), w_vt.astype(BF16), k_tab)

    def q_args(l):
        j = l - n_a
        w_qb = q_b_w[j].reshape(Q_LORA, N_HEADS // 2, 2, QK_NOPE + QK_ROPE)
        w_nope = w_qb[..., :QK_NOPE].reshape(Q_LORA, N_HEADS // 2, 2 * QK_NOPE)
        w_rope = w_qb[..., QK_NOPE:]
        w_swap = _half_swap(w_rope).reshape(Q_LORA, N_HEADS // 2, 2 * QK_ROPE)
        w_rope = w_rope.reshape(Q_LORA, N_HEADS // 2, 2 * QK_ROPE)
        w_qb = jnp.concatenate([w_nope, w_rope, w_swap], axis=-1)
        w_qb = w_qb.reshape(Q_LORA, N_HEADS * 2 * LANES)
        return (mod[l], norm_g[l], q_a_w[j].astype(BF16), q_a_norm_g[j],
                w_qb.astype(BF16), q_tab)

    def mixer_weights(l):
        if l < n_a:
            return [(conv_in_w, l), (conv_out_w, l)]
        return [(attn_o_w, l - n_a)]

    mix_w = [w[l].astype(BF16) for w, l in mixer_weights(0)]
    k_nope = v_t = k_pe = q_cat = None
    for l in range(depth):
        mlp_casts = [(mlp_up_w, l), (mlp_down_w, l)]
        next_casts = mixer_weights(l + 1) if l + 1 < depth else []
        oproj = None
        if l < n_a:
            x, w_up, w_down = _conv_layer(x, mod[l], norm_g[l], mix_w[0], conv_w[l],
                                          conv_b[l], mix_w[1], mlp_casts)
        else:
            attn = _attention(q_cat, k_nope, k_pe, v_t)
            oproj = (attn, mix_w[0])
            w_up, w_down = mlp_w
        if l + 1 >= n_a:
            next_casts = next_casts + ([(mlp_up_w, l + 1), (mlp_down_w, l + 1)]
                                       if l + 1 < depth else [])
        kv = kv_args if l + 1 == n_a else None
        q = q_args(l + 1) if n_a <= l + 1 < depth else None
        outs = _mlp_layer(x, mod[l], norm_g[l], w_up, w_down, oproj, kv, q,
                          next_casts)
        x = outs[0]
        if kv is not None:
            k_nope, v_t, k_pe = outs[1:4]
        if q is not None:
            q_cat = outs[4] if kv is not None else outs[1]
        cast_outs = list(outs[len(outs) - len(next_casts):]) if next_casts else []
        n_mix = len(mixer_weights(l + 1)) if l + 1 < depth else 0
        mix_w, mlp_w = cast_outs[:n_mix], cast_outs[n_mix:]
    return x
```

```python
import functools
import math

import jax
import jax.numpy as jnp
import numpy as np
from jax import lax
from jax.experimental import pallas as pl
from jax.experimental.pallas import tpu as pltpu

CHUNK = 64
CONV_W = 3
N_HEADS = 8
QK_NOPE = 128
QK_ROPE = 64
V_HEAD = 128
Q_LORA = 384
KV_LORA = 256
ROPE_THETA = 10000.0
NORM_EPS = 1e-6

V7X_VMEM_LIMIT_BYTES = 56 * 1024 * 1024
LANES = 128

TOKEN_TILE = 512
FF_CHUNK = 1024
ADA_COL_TILE = 1536
Q_TILE = 256
ROW_SUB = 256
HEADS_PER_STEP = 4

F32 = jnp.float32
BF16 = jnp.bfloat16

Q_SCALE = math.log2(math.e) / math.sqrt(QK_NOPE + QK_ROPE)


def _normalize(x):
    ms = jnp.mean(x * x, axis=-1, keepdims=True)
    return x * lax.rsqrt(ms + NORM_EPS)


def _rms(x, g):
    return _normalize(x) * g


def _mod_gain(g, scale):
    return g * (1.0 + scale)


def _dot(a, b):
    return jnp.dot(a, b, preferred_element_type=F32)


def _dot_nt(a, b):
    return lax.dot_general(a, b, (((1,), (1,)), ((), ())),
                           preferred_element_type=F32)


def _const_spec(shape):
    nd = len(shape)
    return pl.BlockSpec(shape, lambda *_: (0,) * nd,
                        pipeline_mode=pl.Buffered(1))


def _params(n_axes):
    return pltpu.CompilerParams(
        dimension_semantics=("arbitrary",) * n_axes,
        vmem_limit_bytes=V7X_VMEM_LIMIT_BYTES)


def _cast_jobs(weights, grid):
    n_b, n_i = grid
    args, in_specs, out_specs, out_shapes = [], [], [], []
    for w, layer in weights:
        _, rows, cols = w.shape
        rb = rows // (n_b * n_i)
        assert rb * n_b * n_i == rows and rb % 16 == 0, (w.shape, grid)
        args.append(w)
        in_specs.append(pl.BlockSpec(
            (1, rb, cols), lambda b, i, layer=layer: (layer, b * n_i + i, 0)))
        out_specs.append(pl.BlockSpec((rb, cols), lambda b, i: (b * n_i + i, 0)))
        out_shapes.append(jax.ShapeDtypeStruct((rows, cols), BF16))
    return args, in_specs, out_specs, out_shapes


def _run_cast_jobs(src_refs, dst_refs):
    for src, dst in zip(src_refs, dst_refs):
        dst[...] = src[0].astype(BF16)


def _ada_kernel(c_ref, w_ref, b_ref, o_ref):
    c = c_ref[...]
    c_act = (c * jax.nn.sigmoid(c)).astype(BF16)
    o_ref[0] = _dot(c_act, w_ref[0].astype(BF16)) + b_ref[0]


def _ada(c, w, b):
    n_layers, d, n = w.shape
    bsz = c.shape[0]
    tn = max(t for t in range(LANES, ADA_COL_TILE + 1, LANES) if n % t == 0)
    return pl.pallas_call(
        _ada_kernel,
        out_shape=jax.ShapeDtypeStruct((n_layers, bsz, n), F32),
        grid=(n_layers, n // tn),
        in_specs=[
            _const_spec((bsz, d)),
            pl.BlockSpec((1, d, tn), lambda l, j: (l, 0, j)),
            pl.BlockSpec((1, 1, tn), lambda l, j: (l, 0, j)),
        ],
        out_specs=pl.BlockSpec((1, bsz, tn), lambda l, j: (l, 0, j)),
        compiler_params=_params(2),
        name="ada_mod",
    )(c, w, b.reshape(n_layers, 1, n))


def _conv_kernel(*refs, n_casts):
    x_ref, mod_ref, g_ref, win_ref, cw_ref, cb_ref, wout_ref = refs[:7]
    cast_src = refs[7:7 + n_casts]
    o_ref = refs[7 + n_casts]
    cast_dst = refs[8 + n_casts:8 + 2 * n_casts]
    u_ref = refs[8 + 2 * n_casts]
    _run_cast_jobs(cast_src, cast_dst)
    tm, d = x_ref.shape[1], x_ref.shape[2]
    rs = ROW_SUB
    mod = mod_ref[0]
    shift, gate = mod[:, 0:d], mod[:, 2 * d:3 * d]
    gain = _mod_gain(g_ref[0:1, :], mod[:, d:2 * d])

    @pl.when(pl.program_id(1) == 0)
    def _():
        u_ref[0:8, :] = jnp.zeros((8, d), F32)

    @pl.when(pl.program_id(1) > 0)
    def _():
        u_ref[0:8, :] = u_ref[tm:tm + 8, :]

    def project(r):
        x = x_ref[0, r * rs:(r + 1) * rs, :]
        h = _normalize(x) * gain + shift
        return _dot(h.astype(BF16), win_ref[...])

    def mix(r, bcx):
        lo = 8 + r * rs
        u = bcx[:, d:2 * d] * bcx[:, 2 * d:3 * d]
        u_ref[lo:lo + rs, :] = u
        conv = (cw_ref[0:1, :] * u_ref[lo - 2:lo - 2 + rs, :]
                + cw_ref[1:2, :] * u_ref[lo - 1:lo - 1 + rs, :]
                + cw_ref[2:3, :] * u
                + cb_ref[...])
        y = _dot((bcx[:, 0:d] * conv).astype(BF16), wout_ref[...])
        rows = slice(r * rs, (r + 1) * rs)
        o_ref[0, rows, :] = x_ref[0, rows, :] + gate * _rms(y, g_ref[1:2, :])

    nxt = project(0)
    for r in range(tm // rs):
        cur = nxt
        if r + 1 < tm // rs:
            nxt = project(r + 1)
        mix(r, cur)


def _conv_layer(x, mod, g, w_in, conv_w, conv_b, w_out, casts=()):
    bsz, s, d = x.shape
    tm = TOKEN_TILE
    grid = (bsz, s // tm)
    c_args, c_in, c_out, c_shapes = _cast_jobs(casts, grid)
    return pl.pallas_call(
        functools.partial(_conv_kernel, n_casts=len(c_args)),
        out_shape=(jax.ShapeDtypeStruct(x.shape, F32), *c_shapes),
        grid=grid,
        in_specs=[
            pl.BlockSpec((1, tm, d), lambda b, i: (b, i, 0)),
            pl.BlockSpec((1, 1, 6 * d), lambda b, i: (b, 0, 0)),
            _const_spec(g.shape),
            _const_spec(w_in.shape),
            _const_spec(conv_w.shape),
            _const_spec((1, d)),
            _const_spec(w_out.shape),
            *c_in,
        ],
        out_specs=(pl.BlockSpec((1, tm, d), lambda b, i: (b, i, 0)), *c_out),
        scratch_shapes=[pltpu.VMEM((tm + 8, d), F32)],
        compiler_params=_params(2),
        name="conv_mixer",
    )(x, mod, g, w_in, conv_w, conv_b.reshape(1, d), w_out, *c_args)


def _kv_tail(xn, rows, mod_ref, g_ref, wa_ref, ga_ref, wk_ref, wvt_ref, tab_ref,
             kn_ref, vt_ref, kpe_ref):
    d = xn.shape[1]
    mod = mod_ref[0]
    h = xn * _mod_gain(g_ref[...], mod[:, d:2 * d]) + mod[:, 0:d]
    kva = _dot(h.astype(BF16), wa_ref[...])
    c_kv = _rms(kva[:, 0:KV_LORA], ga_ref[...]).astype(BF16)
    pe = kva[:, KV_LORA:KV_LORA + LANES]
    pe_sw = pltpu.roll(pe, QK_ROPE, 1)
    tab = tab_ref[rows, :]
    kpe_ref[0, rows, :] = (pe * tab[:, 0:LANES]
                           + pe_sw * tab[:, LANES:2 * LANES]).astype(BF16)
    kn = _dot(c_kv, wk_ref[...]).astype(BF16)
    for hd in range(N_HEADS):
        kn_ref[0, hd, rows, :] = kn[:, hd * QK_NOPE:(hd + 1) * QK_NOPE]
    vt_ref[0, :, rows] = _dot_nt(wvt_ref[...], c_kv).astype(BF16)


def _q_tail(xn, rows, mod_ref, g_ref, wa_ref, ga_ref, wbt_ref, tab_ref, q_ref):
    d = xn.shape[1]
    n_rows = xn.shape[0]
    hw = 2 * LANES
    mod = mod_ref[0]
    h = xn * _mod_gain(g_ref[0:1, :], mod[:, d:2 * d]) + mod[:, 0:d]
    qa = _rms(_dot(h.astype(BF16), wa_ref[...]), ga_ref[...]).astype(BF16)
    half = N_HEADS * hw // 2
    qt = [_dot_nt(wbt_ref[0:half, :], qa), _dot_nt(wbt_ref[half:2 * half, :], qa)]
    tab = tab_ref[:, rows]
    cos_t, sin_t = tab[0:QK_ROPE, :], tab[QK_ROPE:2 * QK_ROPE, :]
    zeros = jnp.zeros((hw - QK_NOPE - QK_ROPE, n_rows), BF16)
    for hd in range(N_HEADS):
        part, r0 = qt[hd * hw // half], (hd * hw) % half
        blk = part[r0:r0 + hw, :]
        rope = (blk[QK_NOPE:QK_NOPE + QK_ROPE, :] * cos_t
                + blk[QK_NOPE + QK_ROPE:hw, :] * sin_t)
        q_ref[0, hd, 0:QK_NOPE, rows] = (blk[0:QK_NOPE, :] * Q_SCALE).astype(BF16)
        q_ref[0, hd, QK_NOPE:QK_NOPE + QK_ROPE, rows] = rope.astype(BF16)
        q_ref[0, hd, QK_NOPE + QK_ROPE:hw, rows] = zeros


N_KV_IN, N_KV_OUT = 7, 3
N_Q_IN, N_Q_OUT = 6, 1


def _mlp_kernel(*refs, with_oproj, with_kv, with_q, n_casts):
    refs = list(refs)
    x_ref, mod_ref, g_ref, up_ref, down_ref = refs[:5]
    pos = 5
    if with_oproj:
        a_ref, ow_ref = refs[pos:pos + 2]
        pos += 2
    if with_kv:
        kv_in = refs[pos:pos + N_KV_IN]
        pos += N_KV_IN
    if with_q:
        q_in = refs[pos:pos + N_Q_IN]
        pos += N_Q_IN
    cast_src = refs[pos:pos + n_casts]
    pos += n_casts
    o_ref = refs[pos]
    pos += 1
    if with_kv:
        kv_out = refs[pos:pos + N_KV_OUT]
        pos += N_KV_OUT
    if with_q:
        q_out = refs[pos:pos + N_Q_OUT]
        pos += N_Q_OUT
    cast_dst = refs[pos:pos + n_casts]
    _run_cast_jobs(cast_src, cast_dst)

    tm, d = x_ref.shape[1], x_ref.shape[2]
    d_ff = up_ref.shape[1]
    rs = ROW_SUB
    n_sub = tm // rs
    mod = mod_ref[0]
    gate_m = mod[:, 2 * d:3 * d]
    shift, gate = mod[:, 3 * d:4 * d], mod[:, 5 * d:6 * d]
    gain = _mod_gain(g_ref[2:3, :], mod[:, 4 * d:5 * d])
    chunks = list(range(0, d_ff, FF_CHUNK))

    def prologue(r):
        rows = slice(r * rs, (r + 1) * rs)
        x = x_ref[0, rows, :]
        if with_oproj:
            a = jnp.concatenate([a_ref[0, hd, rows, :] for hd in range(N_HEADS)],
                                axis=1)
            y = _dot(a, ow_ref[...])
            x = x + gate_m * _rms(y, g_ref[1:2, :])
        h = (_normalize(x) * gain + shift).astype(BF16)
        return x, h

    def ff_chunk(h, k, y):
        hid = _dot(h, up_ref[:, k:k + FF_CHUNK])
        act = jnp.square(jnp.maximum(hid, 0.0)).astype(BF16)
        part = _dot(act, down_ref[k:k + FF_CHUNK, :])
        return part if y is None else y + part

    def epilogue(r, x, y):
        rows = slice(r * rs, (r + 1) * rs)
        x_new = x + gate * _rms(y, g_ref[3:4, :])
        o_ref[0, rows, :] = x_new
        if with_kv or with_q:
            xn = _normalize(x_new)
        if with_kv:
            _kv_tail(xn, rows, *kv_in, *kv_out)
        if with_q:
            _q_tail(xn, rows, *q_in, *q_out)

    x, h = prologue(0)
    pending = None
    for r in range(n_sub):
        y = None
        for ci, k in enumerate(chunks):
            if ci == len(chunks) - 1 and r + 1 < n_sub:
                nxt = prologue(r + 1)
            y = ff_chunk(h, k, y)
            if ci == 0 and pending is not None:
                epilogue(*pending)
                pending = None
        pending = (r, x, y)
        if r + 1 < n_sub:
            x, h = nxt
    epilogue(*pending)


def _mlp_layer(x, mod, g, w_up, w_down, oproj=None, kv=None, q=None, casts=()):
    bsz, s, d = x.shape
    tm = TOKEN_TILE
    grid = (bsz, s // tm)
    c_args, c_in, c_out, c_shapes = _cast_jobs(casts, grid)
    row = lambda n: pl.BlockSpec((1, tm, n), lambda b, i: (b, i, 0))
    vec = lambda n: pl.BlockSpec((1, 1, n), lambda b, i: (b, 0, 0))
    table = lambda n: pl.BlockSpec((tm, n), lambda b, i: (i, 0))
    heads = lambda n: pl.BlockSpec((1, N_HEADS, tm, n), lambda b, i: (b, 0, i, 0))
    in_specs = [row(d), vec(6 * d), _const_spec(g.shape),
                _const_spec(w_up.shape), _const_spec(w_down.shape)]
    args = [x, mod, g, w_up, w_down]
    out_specs = [row(d)]
    out_shape = [jax.ShapeDtypeStruct(x.shape, F32)]
    name = "mlp"
    if oproj is not None:
        attn, w_o = oproj
        in_specs += [heads(attn.shape[3]), _const_spec(w_o.shape)]
        args += [attn, w_o]
        name = "oproj_" + name
    if kv is not None:
        assert len(kv) == N_KV_IN
        kv_mod, kv_g, w_a, g_a, w_k, w_vt, rope_tab = kv
        n_v = w_vt.shape[0]
        in_specs += [vec(2 * d), _const_spec((1, d)), _const_spec(w_a.shape),
                     _const_spec((1, KV_LORA)), _const_spec(w_k.shape),
                     _const_spec(w_vt.shape), table(2 * LANES)]
        args += [kv_mod, kv_g.reshape(1, d), w_a, g_a.reshape(1, KV_LORA), w_k,
                 w_vt, rope_tab]
        out_specs += [heads(QK_NOPE),
                      pl.BlockSpec((1, n_v, tm), lambda b, i: (b, 0, i)),
                      row(LANES)]
        out_shape += [jax.ShapeDtypeStruct((bsz, N_HEADS, s, QK_NOPE), BF16),
                      jax.ShapeDtypeStruct((bsz, n_v, s), BF16),
                      jax.ShapeDtypeStruct((bsz, s, LANES), BF16)]
        name += "_kv"
    if q is not None:
        assert len(q) == N_Q_IN
        q_mod, q_g, w_a, g_a, w_b, rope_tab = q
        in_specs += [vec(6 * d), _const_spec(q_g.shape), _const_spec(w_a.shape),
                     _const_spec((1, Q_LORA)), _const_spec(w_b.shape),
                     pl.BlockSpec((2 * QK_ROPE, tm), lambda b, i: (0, i))]
        args += [q_mod, q_g, w_a, g_a.reshape(1, Q_LORA), w_b, rope_tab]
        out_specs += [pl.BlockSpec((1, N_HEADS, 2 * LANES, tm),
                                   lambda b, i: (b, 0, 0, i))]
        out_shape += [jax.ShapeDtypeStruct((bsz, N_HEADS, 2 * LANES, s), BF16)]
        name += "_q"
    return pl.pallas_call(
        functools.partial(_mlp_kernel, with_oproj=oproj is not None,
                          with_kv=kv is not None, with_q=q is not None,
                          n_casts=len(c_args)),
        out_shape=(*out_shape, *c_shapes),
        grid=grid,
        in_specs=in_specs + c_in,
        out_specs=(*out_specs, *c_out),
        compiler_params=_params(2),
        name=name,
    )(*args, *c_args)


def _col_reduce(x, op, final):
    n = x.shape[0]
    while n > 8 and n % 16 == 0:
        n //= 2
        x = op(x[:n], x[n:])
    return final(x, axis=0, keepdims=True)


def _attn_kernel(q_ref, kn_ref, kpe_ref, vt_ref, o_ref, kc_ref, s_ref, p_ref):
    s = q_ref.shape[3]
    tq = Q_TILE
    for hh in range(HEADS_PER_STEP):
        kc_ref[hh, :, 0:LANES] = kn_ref[0, hh]
        kc_ref[hh, :, LANES:2 * LANES] = kpe_ref[0]
    k_chunk = lax.broadcasted_iota(jnp.int32, (tq, tq), 0) // CHUNK
    q_chunk = lax.broadcasted_iota(jnp.int32, (tq, tq), 1) // CHUNK
    diag_mask = k_chunk <= q_chunk
    neg = jnp.finfo(F32).min
    n_tiles = s // tq

    def scores(item, slot):
        hh, i = item
        lo, hi = i * tq, (i + 1) * tq
        q = q_ref[0, hh, :, lo:hi]
        s_d = jnp.where(diag_mask, _dot(kc_ref[hh, lo:hi, :], q), neg)
        s_ref[slot, lo:hi, :] = s_d
        m = _col_reduce(s_d, jnp.maximum, jnp.max)
        for a in range(0, lo, tq):
            s_o = _dot(kc_ref[hh, a:a + tq, :], q)
            s_ref[slot, a:a + tq, :] = s_o
            m = jnp.maximum(m, _col_reduce(s_o, jnp.maximum, jnp.max))
        return m

    def softmax_pv(item, slot, m):
        hh, i = item
        lo, hi = i * tq, (i + 1) * tq
        blocks = [(j * tq, (j + 1) * tq) for j in range(i + 1)]
        l = None
        for a, b in blocks:
            p = jnp.exp2(s_ref[slot, a:b, :] - m)
            lb = _col_reduce(p, jnp.add, jnp.sum)
            l = lb if l is None else l + lb
            p_ref[slot, a:b, :] = p.astype(BF16)
        acc = _dot(vt_ref[0, hh * V_HEAD:(hh + 1) * V_HEAD, 0:hi],
                   p_ref[slot, 0:hi, :])
        o_ref[0, hh, lo:hi, :] = (acc * (1.0 / l)).T.astype(BF16)

    order = list(range(1, n_tiles, 2)) + list(range(n_tiles - 2 + n_tiles % 2, -1, -2))
    items = [(hh, i) for hh in range(HEADS_PER_STEP) for i in order]
    m_next = scores(items[0], 0)
    for pos, item in enumerate(items):
        m = m_next
        if pos + 1 < len(items):
            m_next = scores(items[pos + 1], (pos + 1) % 2)
        softmax_pv(item, pos % 2, m)


def _attention(q_cat, k_nope, k_pe, v_t):
    bsz, _, _, s = q_cat.shape
    hps = HEADS_PER_STEP
    assert N_HEADS % hps == 0
    head = lambda n: pl.BlockSpec((1, hps, s, n), lambda b, h: (b, h, 0, 0))
    return pl.pallas_call(
        _attn_kernel,
        out_shape=jax.ShapeDtypeStruct((bsz, N_HEADS, s, V_HEAD), BF16),
        grid=(bsz, N_HEADS // hps),
        in_specs=[
            pl.BlockSpec((1, hps, 2 * LANES, s), lambda b, h: (b, h, 0, 0)),
            head(QK_NOPE),
            pl.BlockSpec((1, s, LANES), lambda b, h: (b, 0, 0)),
            pl.BlockSpec((1, hps * V_HEAD, s), lambda b, h: (b, h, 0)),
        ],
        out_specs=head(V_HEAD),
        scratch_shapes=[pltpu.VMEM((hps, s, 2 * LANES), BF16),
                        pltpu.VMEM((2, s, Q_TILE), F32),
                        pltpu.VMEM((2, s, Q_TILE), BF16)],
        compiler_params=_params(2),
        name="mla_attention",
    )(q_cat, k_nope, k_pe, v_t)


def _half_swap(w):
    half = w.shape[-1] // 2
    return jnp.concatenate([w[..., half:], w[..., :half]], axis=-1)


def _rope_tables(seq_len):
    half = QK_ROPE // 2
    inv_freq = 1.0 / (ROPE_THETA ** (jnp.arange(0, QK_ROPE, 2, dtype=F32) / QK_ROPE))
    pos = jnp.arange(seq_len, dtype=F32)
    lane = np.arange(2 * LANES)
    k_freq = inv_freq[lane % half]
    k_cos = np.where(lane < QK_ROPE, 1.0, 0.0).astype(np.float32)
    k_sin = np.where((lane >= LANES) & (lane < LANES + half), -1.0,
                     np.where((lane >= LANES + half) & (lane < LANES + QK_ROPE),
                              1.0, 0.0)).astype(np.float32)
    k_ang = pos[:, None] * k_freq[None, :]
    k_tab = k_cos * jnp.cos(k_ang) + k_sin * jnp.sin(k_ang)
    row = np.arange(2 * QK_ROPE)
    q_freq = inv_freq[row % half]
    q_cos = np.where(row < QK_ROPE, 1.0, 0.0).astype(np.float32)
    q_sin = np.where(row < QK_ROPE, 0.0,
                     np.where(row < QK_ROPE + half, -1.0, 1.0)).astype(np.float32)
    q_ang = q_freq[:, None] * pos[None, :]
    q_tab = (q_cos[:, None] * jnp.cos(q_ang) + q_sin[:, None] * jnp.sin(q_ang)) * Q_SCALE
    return k_tab, q_tab


def kernel(x, c, ada_w, ada_b, norm_g, conv_in_w, conv_w, conv_b, conv_out_w,
           kv_ada_w, kv_ada_b, kv_norm_g, kv_a_w, kv_a_norm_g, kv_b_w,
           q_a_w, q_a_norm_g, q_b_w, attn_o_w, mlp_up_w, mlp_down_w):
    bsz, s, d = x.shape
    depth = ada_w.shape[0]
    n_a = conv_in_w.shape[0]
    assert 1 <= n_a < depth, "needs a conv layer before the first attention layer"

    mod = _ada(c, ada_w, ada_b).reshape(depth, bsz, 1, 6 * d)
    kv_mod = _ada(c, kv_ada_w[None], kv_ada_b[None]).reshape(bsz, 1, 2 * d)
    k_tab, q_tab = _rope_tables(s)

    w_pe = kv_a_w[:, KV_LORA:]
    w_kva = jnp.concatenate([kv_a_w[:, :KV_LORA], w_pe, _half_swap(w_pe)], axis=1)
    w_kvb = kv_b_w.reshape(KV_LORA, N_HEADS, QK_NOPE + V_HEAD)
    w_k = w_kvb[:, :, :QK_NOPE].reshape(KV_LORA, -1)
    w_vt = w_kvb[:, :, QK_NOPE:].reshape(KV_LORA, -1).T
    kv_args = (kv_mod, kv_norm_g, w_kva.astype(BF16), kv_a_norm_g,
               w_k.astype(BF16), w_vt.astype(BF16), k_tab)

    def q_args(l):
        j = l - n_a
        w_qb = q_b_w[j].reshape(Q_LORA, N_HEADS, QK_NOPE + QK_ROPE)
        w_qb = jnp.concatenate([w_qb, _half_swap(w_qb[:, :, QK_NOPE:])], axis=-1)
        w_qbt = w_qb.reshape(Q_LORA, N_HEADS * 2 * LANES).T
        return (mod[l], norm_g[l], q_a_w[j].astype(BF16), q_a_norm_g[j],
                w_qbt.astype(BF16), q_tab)

    def mixer_weights(l):
        if l < n_a:
            return [(conv_in_w, l), (conv_out_w, l)]
        return [(attn_o_w, l - n_a)]

    mix_w = [w[l].astype(BF16) for w, l in mixer_weights(0)]
    k_nope = v_t = k_pe = q_cat = None
    for l in range(depth):
        mlp_casts = [(mlp_up_w, l), (mlp_down_w, l)]
        next_casts = mixer_weights(l + 1) if l + 1 < depth else []
        oproj = None
        if l < n_a:
            x, w_up, w_down = _conv_layer(x, mod[l], norm_g[l], mix_w[0], conv_w[l],
                                          conv_b[l], mix_w[1], mlp_casts)
        else:
            attn = _attention(q_cat, k_nope, k_pe, v_t)
            oproj = (attn, mix_w[0])
            w_up, w_down = mlp_w
        if l + 1 >= n_a:
            next_casts = next_casts + ([(mlp_up_w, l + 1), (mlp_down_w, l + 1)]
                                       if l + 1 < depth else [])
        kv = kv_args if l + 1 == n_a else None
        q = q_args(l + 1) if n_a <= l + 1 < depth else None
        outs = _mlp_layer(x, mod[l], norm_g[l], w_up, w_down, oproj, kv, q,
                          next_casts)
        x = outs[0]
        if kv is not None:
            k_nope, v_t, k_pe = outs[1:4]
        if q is not None:
            q_cat = outs[4] if kv is not None else outs[1]
        cast_outs = list(outs[len(outs) - len(next_casts):]) if next_casts else []
        n_mix = len(mixer_weights(l + 1)) if l + 1 < depth else 0
        mix_w, mlp_w = cast_outs[:n_mix], cast_outs[n_mix:]
    return x
```

```python
import functools
import math

import jax
import jax.numpy as jnp
from jax import lax
from jax.experimental import pallas as pl
from jax.experimental.pallas import tpu as pltpu

CHUNK = 64
CONV_W = 3
N_HEADS = 8
QK_NOPE = 128
QK_ROPE = 64
V_HEAD = 128
Q_LORA = 384
KV_LORA = 256
ROPE_THETA = 10000.0
NORM_EPS = 1e-6

V7X_VMEM_LIMIT_BYTES = 56 * 1024 * 1024
LANES = 128

TOKEN_TILE = 512
FF_CHUNK = 1024
ADA_COL_TILE = 1536
Q_TILE = 256
ROW_SUB = 256
HEADS_PER_STEP = 4
SUM_ROWS = 16

F32 = jnp.float32
BF16 = jnp.bfloat16

Q_SCALE = math.log2(math.e) / math.sqrt(QK_NOPE + QK_ROPE)


def _normalize(x):
    ms = jnp.mean(x * x, axis=-1, keepdims=True)
    return x * lax.rsqrt(ms + NORM_EPS)


def _rms(x, g):
    return _normalize(x) * g


def _mod_gain(g, scale):
    return g * (1.0 + scale)


def _dot(a, b):
    return jnp.dot(a, b, preferred_element_type=F32)


def _dot_nt(a, b):
    return lax.dot_general(a, b, (((1,), (1,)), ((), ())),
                           preferred_element_type=F32)


def _const_spec(shape):
    nd = len(shape)
    return pl.BlockSpec(shape, lambda *_: (0,) * nd,
                        pipeline_mode=pl.Buffered(1))


def _params(n_axes):
    return pltpu.CompilerParams(
        dimension_semantics=("arbitrary",) * n_axes,
        vmem_limit_bytes=V7X_VMEM_LIMIT_BYTES)


def _cast_jobs(weights, grid):
    n_b, n_i = grid
    args, in_specs, out_specs, out_shapes = [], [], [], []
    for w, layer in weights:
        _, rows, cols = w.shape
        rb = rows // (n_b * n_i)
        assert rb * n_b * n_i == rows and rb % 16 == 0, (w.shape, grid)
        args.append(w)
        in_specs.append(pl.BlockSpec(
            (1, rb, cols), lambda b, i, layer=layer: (layer, b * n_i + i, 0)))
        out_specs.append(pl.BlockSpec((rb, cols), lambda b, i: (b * n_i + i, 0)))
        out_shapes.append(jax.ShapeDtypeStruct((rows, cols), BF16))
    return args, in_specs, out_specs, out_shapes


def _run_cast_jobs(src_refs, dst_refs):
    for src, dst in zip(src_refs, dst_refs):
        dst[...] = src[0].astype(BF16)


def _ada_kernel(c_ref, w_ref, b_ref, o_ref):
    c = c_ref[...]
    c_act = (c * jax.nn.sigmoid(c)).astype(BF16)
    o_ref[0] = _dot(c_act, w_ref[0].astype(BF16)) + b_ref[0]


def _ada(c, w, b):
    n_layers, d, n = w.shape
    bsz = c.shape[0]
    tn = max(t for t in range(LANES, ADA_COL_TILE + 1, LANES) if n % t == 0)
    return pl.pallas_call(
        _ada_kernel,
        out_shape=jax.ShapeDtypeStruct((n_layers, bsz, n), F32),
        grid=(n_layers, n // tn),
        in_specs=[
            _const_spec((bsz, d)),
            pl.BlockSpec((1, d, tn), lambda l, j: (l, 0, j)),
            pl.BlockSpec((1, 1, tn), lambda l, j: (l, 0, j)),
        ],
        out_specs=pl.BlockSpec((1, bsz, tn), lambda l, j: (l, 0, j)),
        compiler_params=_params(2),
        name="ada_mod",
    )(c, w, b.reshape(n_layers, 1, n))


def _conv_kernel(*refs, n_casts):
    x_ref, mod_ref, g_ref, win_ref, cw_ref, cb_ref, wout_ref = refs[:7]
    cast_src = refs[7:7 + n_casts]
    o_ref = refs[7 + n_casts]
    cast_dst = refs[8 + n_casts:8 + 2 * n_casts]
    u_ref = refs[8 + 2 * n_casts]
    _run_cast_jobs(cast_src, cast_dst)
    tm, d = x_ref.shape[1], x_ref.shape[2]
    rs = ROW_SUB
    mod = mod_ref[0]
    shift, gate = mod[:, 0:d], mod[:, 2 * d:3 * d]
    gain = _mod_gain(g_ref[0:1, :], mod[:, d:2 * d])

    @pl.when(pl.program_id(1) == 0)
    def _():
        u_ref[0:8, :] = jnp.zeros((8, d), F32)

    @pl.when(pl.program_id(1) > 0)
    def _():
        u_ref[0:8, :] = u_ref[tm:tm + 8, :]

    def project(r):
        x = x_ref[0, r * rs:(r + 1) * rs, :]
        h = _normalize(x) * gain + shift
        return _dot(h.astype(BF16), win_ref[...])

    def mix(r, bcx):
        lo = 8 + r * rs
        u = bcx[:, d:2 * d] * bcx[:, 2 * d:3 * d]
        u_ref[lo:lo + rs, :] = u
        conv = (cw_ref[0:1, :] * u_ref[lo - 2:lo - 2 + rs, :]
                + cw_ref[1:2, :] * u_ref[lo - 1:lo - 1 + rs, :]
                + cw_ref[2:3, :] * u
                + cb_ref[...])
        y = _dot((bcx[:, 0:d] * conv).astype(BF16), wout_ref[...])
        rows = slice(r * rs, (r + 1) * rs)
        o_ref[0, rows, :] = x_ref[0, rows, :] + gate * _rms(y, g_ref[1:2, :])

    nxt = project(0)
    for r in range(tm // rs):
        cur = nxt
        if r + 1 < tm // rs:
            nxt = project(r + 1)
        mix(r, cur)


def _conv_layer(x, mod, g, w_in, conv_w, conv_b, w_out, casts=()):
    bsz, s, d = x.shape
    tm = TOKEN_TILE
    grid = (bsz, s // tm)
    c_args, c_in, c_out, c_shapes = _cast_jobs(casts, grid)
    return pl.pallas_call(
        functools.partial(_conv_kernel, n_casts=len(c_args)),
        out_shape=(jax.ShapeDtypeStruct(x.shape, F32), *c_shapes),
        grid=grid,
        in_specs=[
            pl.BlockSpec((1, tm, d), lambda b, i: (b, i, 0)),
            pl.BlockSpec((1, 1, 6 * d), lambda b, i: (b, 0, 0)),
            _const_spec(g.shape),
            _const_spec(w_in.shape),
            _const_spec(conv_w.shape),
            _const_spec((1, d)),
            _const_spec(w_out.shape),
            *c_in,
        ],
        out_specs=(pl.BlockSpec((1, tm, d), lambda b, i: (b, i, 0)), *c_out),
        scratch_shapes=[pltpu.VMEM((tm + 8, d), F32)],
        compiler_params=_params(2),
        name="conv_mixer",
    )(x, mod, g, w_in, conv_w, conv_b.reshape(1, d), w_out, *c_args)


def _kv_tail(xn, rows, mod_ref, g_ref, wa_ref, ga_ref, wk_ref, wvt_ref, tab_ref,
             kn_ref, vt_ref, kpe_ref):
    d = xn.shape[1]
    mod = mod_ref[0]
    h = xn * _mod_gain(g_ref[...], mod[:, d:2 * d]) + mod[:, 0:d]
    kva = _dot(h.astype(BF16), wa_ref[...])
    c_kv = _rms(kva[:, 0:KV_LORA], ga_ref[...]).astype(BF16)
    pe = kva[:, KV_LORA:KV_LORA + LANES]
    prod = pe * tab_ref[rows, :]
    lane = lax.broadcasted_iota(jnp.int32, prod.shape, 1)
    kpe = jnp.where(lane < QK_ROPE, prod + pltpu.roll(prod, QK_ROPE, 1), 0.0)
    kpe_ref[0, rows, :] = kpe.astype(BF16)
    kn = _dot(c_kv, wk_ref[...]).astype(BF16)
    for hd in range(N_HEADS):
        kn_ref[0, hd, rows, :] = kn[:, hd * QK_NOPE:(hd + 1) * QK_NOPE]
    vt_ref[0, :, rows] = _dot_nt(wvt_ref[...], c_kv).astype(BF16)


def _q_tail(xn, rows, mod_ref, g_ref, wa_ref, ga_ref, wbt_ref, tab_ref, q_ref):
    d = xn.shape[1]
    n_rows = xn.shape[0]
    hw = 2 * LANES
    mod = mod_ref[0]
    h = xn * _mod_gain(g_ref[0:1, :], mod[:, d:2 * d]) + mod[:, 0:d]
    qa = _rms(_dot(h.astype(BF16), wa_ref[...]), ga_ref[...]).astype(BF16)
    half = N_HEADS * hw // 2
    qt = [_dot_nt(wbt_ref[0:half, :], qa), _dot_nt(wbt_ref[half:2 * half, :], qa)]
    tab = tab_ref[:, rows]
    cos_t, sin_t = tab[0:QK_ROPE, :], tab[QK_ROPE:2 * QK_ROPE, :]
    zeros = jnp.zeros((hw - QK_NOPE - QK_ROPE, n_rows), BF16)
    for hd in range(N_HEADS):
        part, r0 = qt[hd * hw // half], (hd * hw) % half
        blk = part[r0:r0 + hw, :]
        rope = (blk[QK_NOPE:QK_NOPE + QK_ROPE, :] * cos_t
                + blk[QK_NOPE + QK_ROPE:hw, :] * sin_t)
        q_ref[0, hd, 0:QK_NOPE, rows] = (blk[0:QK_NOPE, :] * Q_SCALE).astype(BF16)
        q_ref[0, hd, QK_NOPE:QK_NOPE + QK_ROPE, rows] = rope.astype(BF16)
        q_ref[0, hd, QK_NOPE + QK_ROPE:hw, rows] = zeros


N_KV_IN, N_KV_OUT = 7, 3
N_Q_IN, N_Q_OUT = 6, 1


def _mlp_kernel(*refs, with_oproj, with_kv, with_q, n_casts):
    refs = list(refs)
    x_ref, mod_ref, g_ref, up_ref, down_ref = refs[:5]
    pos = 5
    if with_oproj:
        a_ref, ow_ref = refs[pos:pos + 2]
        pos += 2
    if with_kv:
        kv_in = refs[pos:pos + N_KV_IN]
        pos += N_KV_IN
    if with_q:
        q_in = refs[pos:pos + N_Q_IN]
        pos += N_Q_IN
    cast_src = refs[pos:pos + n_casts]
    pos += n_casts
    o_ref = refs[pos]
    pos += 1
    if with_kv:
        kv_out = refs[pos:pos + N_KV_OUT]
        pos += N_KV_OUT
    if with_q:
        q_out = refs[pos:pos + N_Q_OUT]
        pos += N_Q_OUT
    cast_dst = refs[pos:pos + n_casts]
    _run_cast_jobs(cast_src, cast_dst)

    tm, d = x_ref.shape[1], x_ref.shape[2]
    d_ff = up_ref.shape[1]
    rs = ROW_SUB
    n_sub = tm // rs
    mod = mod_ref[0]
    gate_m = mod[:, 2 * d:3 * d]
    shift, gate = mod[:, 3 * d:4 * d], mod[:, 5 * d:6 * d]
    gain = _mod_gain(g_ref[2:3, :], mod[:, 4 * d:5 * d])
    chunks = list(range(0, d_ff, FF_CHUNK))

    def prologue(r):
        rows = slice(r * rs, (r + 1) * rs)
        x = x_ref[0, rows, :]
        if with_oproj:
            a = jnp.concatenate([a_ref[0, hd, rows, :] for hd in range(N_HEADS)],
                                axis=1)
            y = _dot(a, ow_ref[...])
            x = x + gate_m * _rms(y, g_ref[1:2, :])
        h = (_normalize(x) * gain + shift).astype(BF16)
        return x, h

    def ff_chunk(h, k, y):
        hid = _dot(h, up_ref[:, k:k + FF_CHUNK])
        act = jnp.square(jnp.maximum(hid, 0.0)).astype(BF16)
        part = _dot(act, down_ref[k:k + FF_CHUNK, :])
        return part if y is None else y + part

    def epilogue(r, x, y):
        rows = slice(r * rs, (r + 1) * rs)
        x_new = x + gate * _rms(y, g_ref[3:4, :])
        o_ref[0, rows, :] = x_new
        if with_kv or with_q:
            xn = _normalize(x_new)
        if with_kv:
            _kv_tail(xn, rows, *kv_in, *kv_out)
        if with_q:
            _q_tail(xn, rows, *q_in, *q_out)

    x, h = prologue(0)
    pending = None
    for r in range(n_sub):
        y = None
        for ci, k in enumerate(chunks):
            if ci == len(chunks) - 1 and r + 1 < n_sub:
                nxt = prologue(r + 1)
            y = ff_chunk(h, k, y)
            if ci == 0 and pending is not None:
                epilogue(*pending)
                pending = None
        pending = (r, x, y)
        if r + 1 < n_sub:
            x, h = nxt
    epilogue(*pending)


def _mlp_layer(x, mod, g, w_up, w_down, oproj=None, kv=None, q=None, casts=()):
    bsz, s, d = x.shape
    tm = TOKEN_TILE
    grid = (bsz, s // tm)
    c_args, c_in, c_out, c_shapes = _cast_jobs(casts, grid)
    row = lambda n: pl.BlockSpec((1, tm, n), lambda b, i: (b, i, 0))
    vec = lambda n: pl.BlockSpec((1, 1, n), lambda b, i: (b, 0, 0))
    table = lambda n: pl.BlockSpec((tm, n), lambda b, i: (i, 0))
    heads = lambda n: pl.BlockSpec((1, N_HEADS, tm, n), lambda b, i: (b, 0, i, 0))
    in_specs = [row(d), vec(6 * d), _const_spec(g.shape),
                _const_spec(w_up.shape), _const_spec(w_down.shape)]
    args = [x, mod, g, w_up, w_down]
    out_specs = [row(d)]
    out_shape = [jax.ShapeDtypeStruct(x.shape, F32)]
    name = "mlp"
    if oproj is not None:
        attn, w_o = oproj
        in_specs += [heads(attn.shape[3]), _const_spec(w_o.shape)]
        args += [attn, w_o]
        name = "oproj_" + name
    if kv is not None:
        assert len(kv) == N_KV_IN
        kv_mod, kv_g, w_a, g_a, w_k, w_vt, rope_tab = kv
        n_v = w_vt.shape[0]
        in_specs += [vec(2 * d), _const_spec((1, d)), _const_spec(w_a.shape),
                     _const_spec((1, KV_LORA)), _const_spec(w_k.shape),
                     _const_spec(w_vt.shape), table(LANES)]
        args += [kv_mod, kv_g.reshape(1, d), w_a, g_a.reshape(1, KV_LORA), w_k,
                 w_vt, rope_tab]
        out_specs += [heads(QK_NOPE),
                      pl.BlockSpec((1, n_v, tm), lambda b, i: (b, 0, i)),
                      row(LANES)]
        out_shape += [jax.ShapeDtypeStruct((bsz, N_HEADS, s, QK_NOPE), BF16),
                      jax.ShapeDtypeStruct((bsz, n_v, s), BF16),
                      jax.ShapeDtypeStruct((bsz, s, LANES), BF16)]
        name += "_kv"
    if q is not None:
        assert len(q) == N_Q_IN
        q_mod, q_g, w_a, g_a, w_b, rope_tab = q
        in_specs += [vec(6 * d), _const_spec(q_g.shape), _const_spec(w_a.shape),
                     _const_spec((1, Q_LORA)), _const_spec(w_b.shape),
                     pl.BlockSpec((2 * QK_ROPE, tm), lambda b, i: (0, i))]
        args += [q_mod, q_g, w_a, g_a.reshape(1, Q_LORA), w_b, rope_tab]
        out_specs += [pl.BlockSpec((1, N_HEADS, 2 * LANES, tm),
                                   lambda b, i: (b, 0, 0, i))]
        out_shape += [jax.ShapeDtypeStruct((bsz, N_HEADS, 2 * LANES, s), BF16)]
        name += "_q"
    return pl.pallas_call(
        functools.partial(_mlp_kernel, with_oproj=oproj is not None,
                          with_kv=kv is not None, with_q=q is not None,
                          n_casts=len(c_args)),
        out_shape=(*out_shape, *c_shapes),
        grid=grid,
        in_specs=in_specs + c_in,
        out_specs=(*out_specs, *c_out),
        compiler_params=_params(2),
        name=name,
    )(*args, *c_args)


def _col_reduce(x, op, final):
    n = x.shape[0]
    while n > 8 and n % 16 == 0:
        n //= 2
        x = op(x[:n], x[n:])
    return final(x, axis=0, keepdims=True)


def _attn_kernel(q_ref, kn_ref, kpe_ref, vt_ref, o_ref, kc_ref, vx_ref, s_ref, p_ref):
    s = q_ref.shape[3]
    tq = Q_TILE
    for hh in range(HEADS_PER_STEP):
        kc_ref[hh, :, 0:LANES] = kn_ref[0, hh]
        kc_ref[hh, :, LANES:2 * LANES] = kpe_ref[0]
        vx_ref[hh, 0:V_HEAD, :] = vt_ref[0, hh * V_HEAD:(hh + 1) * V_HEAD, :]
        vx_ref[hh, V_HEAD:V_HEAD + SUM_ROWS, :] = jnp.ones((SUM_ROWS, s), BF16)
    k_chunk = lax.broadcasted_iota(jnp.int32, (tq, tq), 0) // CHUNK
    q_chunk = lax.broadcasted_iota(jnp.int32, (tq, tq), 1) // CHUNK
    diag_mask = k_chunk <= q_chunk
    neg = jnp.finfo(F32).min
    n_tiles = s // tq

    def scores(item, slot):
        hh, i = item
        lo, hi = i * tq, (i + 1) * tq
        q = q_ref[0, hh, :, lo:hi]
        s_d = jnp.where(diag_mask, _dot(kc_ref[hh, lo:hi, :], q), neg)
        s_ref[slot, lo:hi, :] = s_d
        m = _col_reduce(s_d, jnp.maximum, jnp.max)
        for a in range(0, lo, tq):
            s_o = _dot(kc_ref[hh, a:a + tq, :], q)
            s_ref[slot, a:a + tq, :] = s_o
            m = jnp.maximum(m, _col_reduce(s_o, jnp.maximum, jnp.max))
        return m

    def softmax_pv(item, slot, m):
        hh, i = item
        lo, hi = i * tq, (i + 1) * tq
        blocks = [(j * tq, (j + 1) * tq) for j in range(i + 1)]
        for a, b in blocks:
            p_ref[slot, a:b, :] = jnp.exp2(s_ref[slot, a:b, :] - m).astype(BF16)
        acc = _dot(vx_ref[hh, :, 0:hi], p_ref[slot, 0:hi, :])
        l = acc[V_HEAD:V_HEAD + 1, :]
        o_ref[0, hh, lo:hi, :] = (acc[0:V_HEAD, :] * (1.0 / l)).T.astype(BF16)

    order = list(range(1, n_tiles, 2)) + list(range(n_tiles - 2 + n_tiles % 2, -1, -2))
    items = [(hh, i) for hh in range(HEADS_PER_STEP) for i in order]
    m_next = scores(items[0], 0)
    for pos, item in enumerate(items):
        m = m_next
        if pos + 1 < len(items):
            m_next = scores(items[pos + 1], (pos + 1) % 2)
        softmax_pv(item, pos % 2, m)


def _attention(q_cat, k_nope, k_pe, v_t):
    bsz, _, _, s = q_cat.shape
    hps = HEADS_PER_STEP
    assert N_HEADS % hps == 0
    head = lambda n: pl.BlockSpec((1, hps, s, n), lambda b, h: (b, h, 0, 0))
    return pl.pallas_call(
        _attn_kernel,
        out_shape=jax.ShapeDtypeStruct((bsz, N_HEADS, s, V_HEAD), BF16),
        grid=(bsz, N_HEADS // hps),
        in_specs=[
            pl.BlockSpec((1, hps, 2 * LANES, s), lambda b, h: (b, h, 0, 0)),
            head(QK_NOPE),
            pl.BlockSpec((1, s, LANES), lambda b, h: (b, 0, 0)),
            pl.BlockSpec((1, hps * V_HEAD, s), lambda b, h: (b, h, 0)),
        ],
        out_specs=head(V_HEAD),
        scratch_shapes=[pltpu.VMEM((hps, s, 2 * LANES), BF16),
                        pltpu.VMEM((hps, V_HEAD + SUM_ROWS, s), BF16),
                        pltpu.VMEM((2, s, Q_TILE), F32),
                        pltpu.VMEM((2, s, Q_TILE), BF16)],
        compiler_params=_params(2),
        name="mla_attention",
    )(q_cat, k_nope, k_pe, v_t)


def _half_swap(w):
    half = w.shape[-1] // 2
    return jnp.concatenate([w[..., half:], w[..., :half]], axis=-1)


def _rope_tables(seq_len):
    inv_freq = 1.0 / (ROPE_THETA ** (jnp.arange(0, QK_ROPE, 2, dtype=F32) / QK_ROPE))
    ang_t = inv_freq[:, None] * jnp.arange(seq_len, dtype=F32)[None, :]
    cos_t, sin_t = jnp.cos(ang_t), jnp.sin(ang_t)
    tab_t = jnp.concatenate([cos_t, cos_t, -sin_t, sin_t], axis=0)
    return tab_t.T, tab_t * Q_SCALE


def kernel(x, c, ada_w, ada_b, norm_g, conv_in_w, conv_w, conv_b, conv_out_w,
           kv_ada_w, kv_ada_b, kv_norm_g, kv_a_w, kv_a_norm_g, kv_b_w,
           q_a_w, q_a_norm_g, q_b_w, attn_o_w, mlp_up_w, mlp_down_w):
    bsz, s, d = x.shape
    depth = ada_w.shape[0]
    n_a = conv_in_w.shape[0]
    assert 1 <= n_a < depth, "needs a conv layer before the first attention layer"

    mod = _ada(c, ada_w, ada_b).reshape(depth, bsz, 1, 6 * d)
    kv_mod = _ada(c, kv_ada_w[None], kv_ada_b[None]).reshape(bsz, 1, 2 * d)
    k_tab, q_tab = _rope_tables(s)

    w_pe = kv_a_w[:, KV_LORA:]
    w_kva = jnp.concatenate([kv_a_w[:, :KV_LORA], w_pe, _half_swap(w_pe)], axis=1)
    w_kvb = kv_b_w.reshape(KV_LORA, N_HEADS, QK_NOPE + V_HEAD)
    w_k = w_kvb[:, :, :QK_NOPE].reshape(KV_LORA, -1)
    w_vt = w_kvb[:, :, QK_NOPE:].reshape(KV_LORA, -1).T
    kv_args = (kv_mod, kv_norm_g, w_kva.astype(BF16), kv_a_norm_g,
               w_k.astype(BF16), w_vt.astype(BF16), k_tab)

    def q_args(l):
        j = l - n_a
        w_qb = q_b_w[j].reshape(Q_LORA, N_HEADS, QK_NOPE + QK_ROPE)
        w_qb = jnp.concatenate([w_qb, _half_swap(w_qb[:, :, QK_NOPE:])], axis=-1)
        w_qbt = w_qb.reshape(Q_LORA, N_HEADS * 2 * LANES).T
        return (mod[l], norm_g[l], q_a_w[j].astype(BF16), q_a_norm_g[j],
                w_qbt.astype(BF16), q_tab)

    def mixer_weights(l):
        if l < n_a:
            return [(conv_in_w, l), (conv_out_w, l)]
        return [(attn_o_w, l - n_a)]

    mix_w = [w[l].astype(BF16) for w, l in mixer_weights(0)]
    k_nope = v_t = k_pe = q_cat = None
    for l in range(depth):
        mlp_casts = [(mlp_up_w, l), (mlp_down_w, l)]
        next_casts = mixer_weights(l + 1) if l + 1 < depth else []
        oproj = None
        if l < n_a:
            x, w_up, w_down = _conv_layer(x, mod[l], norm_g[l], mix_w[0], conv_w[l],
                                          conv_b[l], mix_w[1], mlp_casts)
        else:
            attn = _attention(q_cat, k_nope, k_pe, v_t)
            oproj = (attn, mix_w[0])
            w_up, w_down = mlp_w
        if l + 1 >= n_a:
            next_casts = next_casts + ([(mlp_up_w, l + 1), (mlp_down_w, l + 1)]
                                       if l + 1 < depth else [])
        kv = kv_args if l + 1 == n_a else None
        q = q_args(l + 1) if n_a <= l + 1 < depth else None
        outs = _mlp_layer(x, mod[l], norm_g[l], w_up, w_down, oproj, kv, q,
                          next_casts)
        x = outs[0]
        if kv is not None:
            k_nope, v_t, k_pe = outs[1:4]
        if q is not None:
            q_cat = outs[4] if kv is not None else outs[1]
        cast_outs = list(outs[len(outs) - len(next_casts):]) if next_casts else []
        n_mix = len(mixer_weights(l + 1)) if l + 1 < depth else 0
        mix_w, mlp_w = cast_outs[:n_mix], cast_outs[n_mix:]
    return x
```

```python
import functools
import math

import jax
import jax.numpy as jnp
from jax import lax
from jax.experimental import pallas as pl
from jax.experimental.pallas import tpu as pltpu

CHUNK = 64
CONV_W = 3
N_HEADS = 8
QK_NOPE = 128
QK_ROPE = 64
V_HEAD = 128
Q_LORA = 384
KV_LORA = 256
ROPE_THETA = 10000.0
NORM_EPS = 1e-6

V7X_VMEM_LIMIT_BYTES = 56 * 1024 * 1024
LANES = 128

TOKEN_TILE = 512
FF_CHUNK = 1024
ADA_COL_TILE = 1536
Q_TILE = 256
ROW_SUB = 256
HEADS_PER_STEP = 4

F32 = jnp.float32
BF16 = jnp.bfloat16

Q_SCALE = math.log2(math.e) / math.sqrt(QK_NOPE + QK_ROPE)


def _normalize(x):
    ms = jnp.mean(x * x, axis=-1, keepdims=True)
    return x * lax.rsqrt(ms + NORM_EPS)


def _rms(x, g):
    return _normalize(x) * g


def _mod_gain(g, scale):
    return g * (1.0 + scale)


def _dot(a, b):
    return jnp.dot(a, b, preferred_element_type=F32)


def _dot_nt(a, b):
    return lax.dot_general(a, b, (((1,), (1,)), ((), ())),
                           preferred_element_type=F32)


def _const_spec(shape):
    nd = len(shape)
    return pl.BlockSpec(shape, lambda *_: (0,) * nd,
                        pipeline_mode=pl.Buffered(1))


def _params(n_axes):
    return pltpu.CompilerParams(
        dimension_semantics=("arbitrary",) * n_axes,
        vmem_limit_bytes=V7X_VMEM_LIMIT_BYTES)


def _cast_jobs(weights, grid):
    n_b, n_i = grid
    args, in_specs, out_specs, out_shapes = [], [], [], []
    for w, layer in weights:
        _, rows, cols = w.shape
        rb = rows // (n_b * n_i)
        assert rb * n_b * n_i == rows and rb % 16 == 0, (w.shape, grid)
        args.append(w)
        in_specs.append(pl.BlockSpec(
            (1, rb, cols), lambda b, i, layer=layer: (layer, b * n_i + i, 0)))
        out_specs.append(pl.BlockSpec((rb, cols), lambda b, i: (b * n_i + i, 0)))
        out_shapes.append(jax.ShapeDtypeStruct((rows, cols), BF16))
    return args, in_specs, out_specs, out_shapes


def _run_cast_jobs(src_refs, dst_refs):
    for src, dst in zip(src_refs, dst_refs):
        dst[...] = src[0].astype(BF16)


def _ada_kernel(c_ref, w_ref, b_ref, o_ref):
    c = c_ref[...]
    c_act = (c * jax.nn.sigmoid(c)).astype(BF16)
    o_ref[0] = _dot(c_act, w_ref[0].astype(BF16)) + b_ref[0]


def _ada(c, w, b):
    n_layers, d, n = w.shape
    bsz = c.shape[0]
    tn = max(t for t in range(LANES, ADA_COL_TILE + 1, LANES) if n % t == 0)
    return pl.pallas_call(
        _ada_kernel,
        out_shape=jax.ShapeDtypeStruct((n_layers, bsz, n), F32),
        grid=(n_layers, n // tn),
        in_specs=[
            _const_spec((bsz, d)),
            pl.BlockSpec((1, d, tn), lambda l, j: (l, 0, j)),
            pl.BlockSpec((1, 1, tn), lambda l, j: (l, 0, j)),
        ],
        out_specs=pl.BlockSpec((1, bsz, tn), lambda l, j: (l, 0, j)),
        compiler_params=_params(2),
        name="ada_mod",
    )(c, w, b.reshape(n_layers, 1, n))


def _conv_kernel(*refs, n_casts):
    x_ref, mod_ref, g_ref, win_ref, cw_ref, cb_ref, wout_ref = refs[:7]
    cast_src = refs[7:7 + n_casts]
    o_ref = refs[7 + n_casts]
    cast_dst = refs[8 + n_casts:8 + 2 * n_casts]
    u_ref = refs[8 + 2 * n_casts]
    _run_cast_jobs(cast_src, cast_dst)
    tm, d = x_ref.shape[1], x_ref.shape[2]
    mod = mod_ref[0]
    shift, gate = mod[:, 0:d], mod[:, 2 * d:3 * d]
    gain = _mod_gain(g_ref[0:1, :], mod[:, d:2 * d])

    @pl.when(pl.program_id(1) == 0)
    def _():
        u_ref[0:8, :] = jnp.zeros((8, d), F32)

    @pl.when(pl.program_id(1) > 0)
    def _():
        u_ref[0:8, :] = u_ref[tm:tm + 8, :]

    x = x_ref[0]
    hb = (_normalize(x) * gain + shift).astype(BF16)
    cx = _dot(hb, win_ref[:, d:3 * d])
    bg = _dot(hb, win_ref[:, 0:d])
    u = cx[:, 0:d] * cx[:, d:2 * d]
    u_ref[8:8 + tm, :] = u
    conv = (cw_ref[0:1, :] * u_ref[6:6 + tm, :]
            + cw_ref[1:2, :] * u_ref[7:7 + tm, :]
            + cw_ref[2:3, :] * u
            + cb_ref[...])
    y = _dot((bg * conv).astype(BF16), wout_ref[...])
    o_ref[0] = x + gate * _rms(y, g_ref[1:2, :])


def _conv_layer(x, mod, g, w_in, conv_w, conv_b, w_out, casts=()):
    bsz, s, d = x.shape
    tm = TOKEN_TILE
    grid = (bsz, s // tm)
    c_args, c_in, c_out, c_shapes = _cast_jobs(casts, grid)
    return pl.pallas_call(
        functools.partial(_conv_kernel, n_casts=len(c_args)),
        out_shape=(jax.ShapeDtypeStruct(x.shape, F32), *c_shapes),
        grid=grid,
        in_specs=[
            pl.BlockSpec((1, tm, d), lambda b, i: (b, i, 0)),
            pl.BlockSpec((1, 1, 6 * d), lambda b, i: (b, 0, 0)),
            _const_spec(g.shape),
            _const_spec(w_in.shape),
            _const_spec(conv_w.shape),
            _const_spec((1, d)),
            _const_spec(w_out.shape),
            *c_in,
        ],
        out_specs=(pl.BlockSpec((1, tm, d), lambda b, i: (b, i, 0)), *c_out),
        scratch_shapes=[pltpu.VMEM((tm + 8, d), F32)],
        compiler_params=_params(2),
        name="conv_mixer",
    )(x, mod, g, w_in, conv_w, conv_b.reshape(1, d), w_out, *c_args)


def _kv_tail(xn, rows, mod_ref, g_ref, wa_ref, ga_ref, wk_ref, wvt_ref, tab_ref,
             kn_ref, vt_ref, kpe_ref):
    d = xn.shape[1]
    mod = mod_ref[0]
    h = xn * _mod_gain(g_ref[...], mod[:, d:2 * d]) + mod[:, 0:d]
    kva = _dot(h.astype(BF16), wa_ref[...])
    c_kv = _rms(kva[:, 0:KV_LORA], ga_ref[...]).astype(BF16)
    pe = kva[:, KV_LORA:KV_LORA + LANES]
    prod = pe * tab_ref[rows, :]
    lane = lax.broadcasted_iota(jnp.int32, prod.shape, 1)
    kpe = jnp.where(lane < QK_ROPE, prod + pltpu.roll(prod, QK_ROPE, 1), 0.0)
    kpe_ref[0, rows, :] = kpe.astype(BF16)
    kn = _dot(c_kv, wk_ref[...]).astype(BF16)
    for hd in range(N_HEADS):
        kn_ref[0, hd, rows, :] = kn[:, hd * QK_NOPE:(hd + 1) * QK_NOPE]
    vt_ref[0, :, rows] = _dot_nt(wvt_ref[...], c_kv).astype(BF16)


def _q_tail(xn, rows, mod_ref, g_ref, wa_ref, ga_ref, wbt_ref, tab_ref, q_ref):
    d = xn.shape[1]
    n_rows = xn.shape[0]
    hw = 2 * LANES
    mod = mod_ref[0]
    h = xn * _mod_gain(g_ref[0:1, :], mod[:, d:2 * d]) + mod[:, 0:d]
    qa = _rms(_dot(h.astype(BF16), wa_ref[...]), ga_ref[...]).astype(BF16)
    half = N_HEADS * hw // 2
    qt = [_dot_nt(wbt_ref[0:half, :], qa), _dot_nt(wbt_ref[half:2 * half, :], qa)]
    tab = tab_ref[:, rows]
    cos_t, sin_t = tab[0:QK_ROPE, :], tab[QK_ROPE:2 * QK_ROPE, :]
    zeros = jnp.zeros((hw - QK_NOPE - QK_ROPE, n_rows), BF16)
    for hd in range(N_HEADS):
        part, r0 = qt[hd * hw // half], (hd * hw) % half
        blk = part[r0:r0 + hw, :]
        rope = (blk[QK_NOPE:QK_NOPE + QK_ROPE, :] * cos_t
                + blk[QK_NOPE + QK_ROPE:hw, :] * sin_t)
        q_ref[0, hd, 0:QK_NOPE, rows] = (blk[0:QK_NOPE, :] * Q_SCALE).astype(BF16)
        q_ref[0, hd, QK_NOPE:QK_NOPE + QK_ROPE, rows] = rope.astype(BF16)
        q_ref[0, hd, QK_NOPE + QK_ROPE:hw, rows] = zeros


N_KV_IN, N_KV_OUT = 7, 3
N_Q_IN, N_Q_OUT = 6, 1


def _mlp_kernel(*refs, with_oproj, with_kv, with_q, n_casts):
    refs = list(refs)
    x_ref, mod_ref, g_ref, up_ref, down_ref = refs[:5]
    pos = 5
    if with_oproj:
        a_ref, ow_ref = refs[pos:pos + 2]
        pos += 2
    if with_kv:
        kv_in = refs[pos:pos + N_KV_IN]
        pos += N_KV_IN
    if with_q:
        q_in = refs[pos:pos + N_Q_IN]
        pos += N_Q_IN
    cast_src = refs[pos:pos + n_casts]
    pos += n_casts
    o_ref = refs[pos]
    pos += 1
    if with_kv:
        kv_out = refs[pos:pos + N_KV_OUT]
        pos += N_KV_OUT
    if with_q:
        q_out = refs[pos:pos + N_Q_OUT]
        pos += N_Q_OUT
    cast_dst = refs[pos:pos + n_casts]
    _run_cast_jobs(cast_src, cast_dst)

    tm, d = x_ref.shape[1], x_ref.shape[2]
    d_ff = up_ref.shape[1]
    rs = ROW_SUB
    n_sub = tm // rs
    mod = mod_ref[0]
    gate_m = mod[:, 2 * d:3 * d]
    shift, gate = mod[:, 3 * d:4 * d], mod[:, 5 * d:6 * d]
    gain = _mod_gain(g_ref[2:3, :], mod[:, 4 * d:5 * d])
    chunks = list(range(0, d_ff, FF_CHUNK))

    def prologue(r):
        rows = slice(r * rs, (r + 1) * rs)
        x = x_ref[0, rows, :]
        if with_oproj:
            a = jnp.concatenate([a_ref[0, hd, rows, :] for hd in range(N_HEADS)],
                                axis=1)
            y = _dot(a, ow_ref[...])
            x = x + gate_m * _rms(y, g_ref[1:2, :])
        h = (_normalize(x) * gain + shift).astype(BF16)
        return x, h

    def ff_chunk(h, k, y):
        hid = _dot(h, up_ref[:, k:k + FF_CHUNK])
        act = jnp.square(jnp.maximum(hid, 0.0)).astype(BF16)
        part = _dot(act, down_ref[k:k + FF_CHUNK, :])
        return part if y is None else y + part

    def epilogue(r, x, y):
        rows = slice(r * rs, (r + 1) * rs)
        x_new = x + gate * _rms(y, g_ref[3:4, :])
        o_ref[0, rows, :] = x_new
        if with_kv or with_q:
            xn = _normalize(x_new)
        if with_kv:
            _kv_tail(xn, rows, *kv_in, *kv_out)
        if with_q:
            _q_tail(xn, rows, *q_in, *q_out)

    x, h = prologue(0)
    pending = None
    for r in range(n_sub):
        y = None
        for ci, k in enumerate(chunks):
            if ci == len(chunks) - 1 and r + 1 < n_sub:
                nxt = prologue(r + 1)
            y = ff_chunk(h, k, y)
            if ci == 0 and pending is not None:
                epilogue(*pending)
                pending = None
        pending = (r, x, y)
        if r + 1 < n_sub:
            x, h = nxt
    epilogue(*pending)


def _mlp_layer(x, mod, g, w_up, w_down, oproj=None, kv=None, q=None, casts=()):
    bsz, s, d = x.shape
    tm = TOKEN_TILE
    grid = (bsz, s // tm)
    c_args, c_in, c_out, c_shapes = _cast_jobs(casts, grid)
    row = lambda n: pl.BlockSpec((1, tm, n), lambda b, i: (b, i, 0))
    vec = lambda n: pl.BlockSpec((1, 1, n), lambda b, i: (b, 0, 0))
    table = lambda n: pl.BlockSpec((tm, n), lambda b, i: (i, 0))
    heads = lambda n: pl.BlockSpec((1, N_HEADS, tm, n), lambda b, i: (b, 0, i, 0))
    in_specs = [row(d), vec(6 * d), _const_spec(g.shape),
                _const_spec(w_up.shape), _const_spec(w_down.shape)]
    args = [x, mod, g, w_up, w_down]
    out_specs = [row(d)]
    out_shape = [jax.ShapeDtypeStruct(x.shape, F32)]
    name = "mlp"
    if oproj is not None:
        attn, w_o = oproj
        in_specs += [heads(attn.shape[3]), _const_spec(w_o.shape)]
        args += [attn, w_o]
        name = "oproj_" + name
    if kv is not None:
        assert len(kv) == N_KV_IN
        kv_mod, kv_g, w_a, g_a, w_k, w_vt, rope_tab = kv
        n_v = w_vt.shape[0]
        in_specs += [vec(2 * d), _const_spec((1, d)), _const_spec(w_a.shape),
                     _const_spec((1, KV_LORA)), _const_spec(w_k.shape),
                     _const_spec(w_vt.shape), table(LANES)]
        args += [kv_mod, kv_g.reshape(1, d), w_a, g_a.reshape(1, KV_LORA), w_k,
                 w_vt, rope_tab]
        out_specs += [heads(QK_NOPE),
                      pl.BlockSpec((1, n_v, tm), lambda b, i: (b, 0, i)),
                      row(LANES)]
        out_shape += [jax.ShapeDtypeStruct((bsz, N_HEADS, s, QK_NOPE), BF16),
                      jax.ShapeDtypeStruct((bsz, n_v, s), BF16),
                      jax.ShapeDtypeStruct((bsz, s, LANES), BF16)]
        name += "_kv"
    if q is not None:
        assert len(q) == N_Q_IN
        q_mod, q_g, w_a, g_a, w_b, rope_tab = q
        in_specs += [vec(6 * d), _const_spec(q_g.shape), _const_spec(w_a.shape),
                     _const_spec((1, Q_LORA)), _const_spec(w_b.shape),
                     pl.BlockSpec((2 * QK_ROPE, tm), lambda b, i: (0, i))]
        args += [q_mod, q_g, w_a, g_a.reshape(1, Q_LORA), w_b, rope_tab]
        out_specs += [pl.BlockSpec((1, N_HEADS, 2 * LANES, tm),
                                   lambda b, i: (b, 0, 0, i))]
        out_shape += [jax.ShapeDtypeStruct((bsz, N_HEADS, 2 * LANES, s), BF16)]
        name += "_q"
    return pl.pallas_call(
        functools.partial(_mlp_kernel, with_oproj=oproj is not None,
                          with_kv=kv is not None, with_q=q is not None,
                          n_casts=len(c_args)),
        out_shape=(*out_shape, *c_shapes),
        grid=grid,
        in_specs=in_specs + c_in,
        out_specs=(*out_specs, *c_out),
        compiler_params=_params(2),
        name=name,
    )(*args, *c_args)


def _col_reduce(x, op, final):
    n = x.shape[0]
    while n > 8 and n % 16 == 0:
        n //= 2
        x = op(x[:n], x[n:])
    return final(x, axis=0, keepdims=True)


def _attn_kernel(q_ref, kn_ref, kpe_ref, vt_ref, o_ref, kc_ref, s_ref, p_ref):
    s = q_ref.shape[3]
    tq = Q_TILE
    for hh in range(HEADS_PER_STEP):
        kc_ref[hh, :, 0:LANES] = kn_ref[0, hh]
        kc_ref[hh, :, LANES:2 * LANES] = kpe_ref[0]
    k_chunk = lax.broadcasted_iota(jnp.int32, (tq, tq), 0) // CHUNK
    q_chunk = lax.broadcasted_iota(jnp.int32, (tq, tq), 1) // CHUNK
    diag_mask = k_chunk <= q_chunk
    neg = jnp.finfo(F32).min
    n_tiles = s // tq

    def scores(item, slot):
        hh, i = item
        lo, hi = i * tq, (i + 1) * tq
        q = q_ref[0, hh, :, lo:hi]
        s_d = jnp.where(diag_mask, _dot(kc_ref[hh, lo:hi, :], q), neg)
        s_ref[slot, lo:hi, :] = s_d
        m = _col_reduce(s_d, jnp.maximum, jnp.max)
        for a in range(0, lo, tq):
            s_o = _dot(kc_ref[hh, a:a + tq, :], q)
            s_ref[slot, a:a + tq, :] = s_o
            m = jnp.maximum(m, _col_reduce(s_o, jnp.maximum, jnp.max))
        return m

    def softmax_pv(item, slot, m):
        hh, i = item
        lo, hi = i * tq, (i + 1) * tq
        blocks = [(j * tq, (j + 1) * tq) for j in range(i + 1)]
        l = None
        for a, b in blocks:
            p = jnp.exp2(s_ref[slot, a:b, :] - m)
            lb = _col_reduce(p, jnp.add, jnp.sum)
            l = lb if l is None else l + lb
            p_ref[slot, a:b, :] = p.astype(BF16)
        acc = _dot(vt_ref[0, hh * V_HEAD:(hh + 1) * V_HEAD, 0:hi],
                   p_ref[slot, 0:hi, :])
        o_ref[0, hh, lo:hi, :] = (acc * (1.0 / l)).T.astype(BF16)

    order = list(range(1, n_tiles, 2)) + list(range(n_tiles - 2 + n_tiles % 2, -1, -2))
    items = [(hh, i) for hh in range(HEADS_PER_STEP) for i in order]
    m_next = scores(items[0], 0)
    for pos, item in enumerate(items):
        m = m_next
        if pos + 1 < len(items):
            m_next = scores(items[pos + 1], (pos + 1) % 2)
        softmax_pv(item, pos % 2, m)


def _attention(q_cat, k_nope, k_pe, v_t):
    bsz, _, _, s = q_cat.shape
    hps = HEADS_PER_STEP
    assert N_HEADS % hps == 0
    head = lambda n: pl.BlockSpec((1, hps, s, n), lambda b, h: (b, h, 0, 0))
    return pl.pallas_call(
        _attn_kernel,
        out_shape=jax.ShapeDtypeStruct((bsz, N_HEADS, s, V_HEAD), BF16),
        grid=(bsz, N_HEADS // hps),
        in_specs=[
            pl.BlockSpec((1, hps, 2 * LANES, s), lambda b, h: (b, h, 0, 0)),
            head(QK_NOPE),
            pl.BlockSpec((1, s, LANES), lambda b, h: (b, 0, 0)),
            pl.BlockSpec((1, hps * V_HEAD, s), lambda b, h: (b, h, 0)),
        ],
        out_specs=head(V_HEAD),
        scratch_shapes=[pltpu.VMEM((hps, s, 2 * LANES), BF16),
                        pltpu.VMEM((2, s, Q_TILE), F32),
                        pltpu.VMEM((2, s, Q_TILE), BF16)],
        compiler_params=_params(2),
        name="mla_attention",
    )(q_cat, k_nope, k_pe, v_t)


def _half_swap(w):
    half = w.shape[-1] // 2
    return jnp.concatenate([w[..., half:], w[..., :half]], axis=-1)


def _rope_tables(seq_len):
    inv_freq = 1.0 / (ROPE_THETA ** (jnp.arange(0, QK_ROPE, 2, dtype=F32) / QK_ROPE))
    ang_t = inv_freq[:, None] * jnp.arange(seq_len, dtype=F32)[None, :]
    cos_t, sin_t = jnp.cos(ang_t), jnp.sin(ang_t)
    tab_t = jnp.concatenate([cos_t, cos_t, -sin_t, sin_t], axis=0)
    return tab_t.T, tab_t * Q_SCALE


def kernel(x, c, ada_w, ada_b, norm_g, conv_in_w, conv_w, conv_b, conv_out_w,
           kv_ada_w, kv_ada_b, kv_norm_g, kv_a_w, kv_a_norm_g, kv_b_w,
           q_a_w, q_a_norm_g, q_b_w, attn_o_w, mlp_up_w, mlp_down_w):
    bsz, s, d = x.shape
    depth = ada_w.shape[0]
    n_a = conv_in_w.shape[0]
    assert 1 <= n_a < depth, "needs a conv layer before the first attention layer"

    mod = _ada(c, ada_w, ada_b).reshape(depth, bsz, 1, 6 * d)
    kv_mod = _ada(c, kv_ada_w[None], kv_ada_b[None]).reshape(bsz, 1, 2 * d)
    k_tab, q_tab = _rope_tables(s)

    w_pe = kv_a_w[:, KV_LORA:]
    w_kva = jnp.concatenate([kv_a_w[:, :KV_LORA], w_pe, _half_swap(w_pe)], axis=1)
    w_kvb = kv_b_w.reshape(KV_LORA, N_HEADS, QK_NOPE + V_HEAD)
    w_k = w_kvb[:, :, :QK_NOPE].reshape(KV_LORA, -1)
    w_vt = w_kvb[:, :, QK_NOPE:].reshape(KV_LORA, -1).T
    kv_args = (kv_mod, kv_norm_g, w_kva.astype(BF16), kv_a_norm_g,
               w_k.astype(BF16), w_vt.astype(BF16), k_tab)

    def q_args(l):
        j = l - n_a
        w_qb = q_b_w[j].reshape(Q_LORA, N_HEADS, QK_NOPE + QK_ROPE)
        w_qb = jnp.concatenate([w_qb, _half_swap(w_qb[:, :, QK_NOPE:])], axis=-1)
        w_qbt = w_qb.reshape(Q_LORA, N_HEADS * 2 * LANES).T
        return (mod[l], norm_g[l], q_a_w[j].astype(BF16), q_a_norm_g[j],
                w_qbt.astype(BF16), q_tab)

    def mixer_weights(l):
        if l < n_a:
            return [(conv_in_w, l), (conv_out_w, l)]
        return [(attn_o_w, l - n_a)]

    mix_w = [w[l].astype(BF16) for w, l in mixer_weights(0)]
    k_nope = v_t = k_pe = q_cat = None
    for l in range(depth):
        mlp_casts = [(mlp_up_w, l), (mlp_down_w, l)]
        next_casts = mixer_weights(l + 1) if l + 1 < depth else []
        oproj = None
        if l < n_a:
            x, w_up, w_down = _conv_layer(x, mod[l], norm_g[l], mix_w[0], conv_w[l],
                                          conv_b[l], mix_w[1], mlp_casts)
        else:
            attn = _attention(q_cat, k_nope, k_pe, v_t)
            oproj = (attn, mix_w[0])
            w_up, w_down = mlp_w
        if l + 1 >= n_a:
            next_casts = next_casts + ([(mlp_up_w, l + 1), (mlp_down_w, l + 1)]
                                       if l + 1 < depth else [])
        kv = kv_args if l + 1 == n_a else None
        q = q_args(l + 1) if n_a <= l + 1 < depth else None
        outs = _mlp_layer(x, mod[l], norm_g[l], w_up, w_down, oproj, kv, q,
                          next_casts)
        x = outs[0]
        if kv is not None:
            k_nope, v_t, k_pe = outs[1:4]
        if q is not None:
            q_cat = outs[4] if kv is not None else outs[1]
        cast_outs = list(outs[len(outs) - len(next_casts):]) if next_casts else []
        n_mix = len(mixer_weights(l + 1)) if l + 1 < depth else 0
        mix_w, mlp_w = cast_outs[:n_mix], cast_outs[n_mix:]
    return x
```

```python
import functools
import math

import jax
import jax.numpy as jnp
from jax import lax
from jax.experimental import pallas as pl
from jax.experimental.pallas import tpu as pltpu

CHUNK = 64
CONV_W = 3
N_HEADS = 8
QK_NOPE = 128
QK_ROPE = 64
V_HEAD = 128
Q_LORA = 384
KV_LORA = 256
ROPE_THETA = 10000.0
NORM_EPS = 1e-6

V7X_VMEM_LIMIT_BYTES = 56 * 1024 * 1024
LANES = 128

TOKEN_TILE = 512
CONV_TOKEN_TILE = 1024
FF_CHUNK = 1024
ADA_COL_TILE = 1536
Q_TILE = 256
ROW_SUB = 256
HEADS_PER_STEP = 4

F32 = jnp.float32
BF16 = jnp.bfloat16

Q_SCALE = math.log2(math.e) / math.sqrt(QK_NOPE + QK_ROPE)


def _normalize(x):
    ms = jnp.mean(x * x, axis=-1, keepdims=True)
    return x * lax.rsqrt(ms + NORM_EPS)


def _rms(x, g):
    return _normalize(x) * g


def _mod_gain(g, scale):
    return g * (1.0 + scale)


def _dot(a, b):
    return jnp.dot(a, b, preferred_element_type=F32)


def _dot_nt(a, b):
    return lax.dot_general(a, b, (((1,), (1,)), ((), ())),
                           preferred_element_type=F32)


def _const_spec(shape):
    nd = len(shape)
    return pl.BlockSpec(shape, lambda *_: (0,) * nd,
                        pipeline_mode=pl.Buffered(1))


def _params(n_axes):
    return pltpu.CompilerParams(
        dimension_semantics=("arbitrary",) * n_axes,
        vmem_limit_bytes=V7X_VMEM_LIMIT_BYTES)


def _cast_jobs(weights, grid):
    n_b, n_i = grid
    args, in_specs, out_specs, out_shapes = [], [], [], []
    for w, layer in weights:
        _, rows, cols = w.shape
        rb = rows // (n_b * n_i)
        assert rb * n_b * n_i == rows and rb % 16 == 0, (w.shape, grid)
        args.append(w)
        in_specs.append(pl.BlockSpec(
            (1, rb, cols), lambda b, i, layer=layer: (layer, b * n_i + i, 0)))
        out_specs.append(pl.BlockSpec((rb, cols), lambda b, i: (b * n_i + i, 0)))
        out_shapes.append(jax.ShapeDtypeStruct((rows, cols), BF16))
    return args, in_specs, out_specs, out_shapes


def _run_cast_jobs(src_refs, dst_refs):
    for src, dst in zip(src_refs, dst_refs):
        dst[...] = src[0].astype(BF16)


def _ada_kernel(c_ref, w_ref, b_ref, o_ref):
    c = c_ref[...]
    c_act = (c * jax.nn.sigmoid(c)).astype(BF16)
    o_ref[0] = _dot(c_act, w_ref[0].astype(BF16)) + b_ref[0]


def _ada(c, w, b):
    n_layers, d, n = w.shape
    bsz = c.shape[0]
    tn = max(t for t in range(LANES, ADA_COL_TILE + 1, LANES) if n % t == 0)
    return pl.pallas_call(
        _ada_kernel,
        out_shape=jax.ShapeDtypeStruct((n_layers, bsz, n), F32),
        grid=(n_layers, n // tn),
        in_specs=[
            _const_spec((bsz, d)),
            pl.BlockSpec((1, d, tn), lambda l, j: (l, 0, j)),
            pl.BlockSpec((1, 1, tn), lambda l, j: (l, 0, j)),
        ],
        out_specs=pl.BlockSpec((1, bsz, tn), lambda l, j: (l, 0, j)),
        compiler_params=_params(2),
        name="ada_mod",
    )(c, w, b.reshape(n_layers, 1, n))


def _conv_kernel(*refs, n_casts):
    x_ref, mod_ref, g_ref, win_ref, cw_ref, cb_ref, wout_ref = refs[:7]
    cast_src = refs[7:7 + n_casts]
    o_ref = refs[7 + n_casts]
    cast_dst = refs[8 + n_casts:8 + 2 * n_casts]
    u_ref = refs[8 + 2 * n_casts]
    _run_cast_jobs(cast_src, cast_dst)
    tm, d = x_ref.shape[1], x_ref.shape[2]
    mod = mod_ref[0]
    shift, gate = mod[:, 0:d], mod[:, 2 * d:3 * d]
    gain = _mod_gain(g_ref[0:1, :], mod[:, d:2 * d])

    @pl.when(pl.program_id(1) == 0)
    def _():
        u_ref[0:8, :] = jnp.zeros((8, d), F32)

    @pl.when(pl.program_id(1) > 0)
    def _():
        u_ref[0:8, :] = u_ref[tm:tm + 8, :]

    x = x_ref[0]
    hb = (_normalize(x) * gain + shift).astype(BF16)
    cx = _dot(hb, win_ref[:, d:3 * d])
    bg = _dot(hb, win_ref[:, 0:d])
    u = cx[:, 0:d] * cx[:, d:2 * d]
    u_ref[8:8 + tm, :] = u
    conv = (cw_ref[0:1, :] * u_ref[6:6 + tm, :]
            + cw_ref[1:2, :] * u_ref[7:7 + tm, :]
            + cw_ref[2:3, :] * u
            + cb_ref[...])
    y = _dot((bg * conv).astype(BF16), wout_ref[...])
    o_ref[0] = x + gate * _rms(y, g_ref[1:2, :])


def _conv_layer(x, mod, g, w_in, conv_w, conv_b, w_out, casts=()):
    bsz, s, d = x.shape
    tm = CONV_TOKEN_TILE
    grid = (bsz, s // tm)
    c_args, c_in, c_out, c_shapes = _cast_jobs(casts, grid)
    return pl.pallas_call(
        functools.partial(_conv_kernel, n_casts=len(c_args)),
        out_shape=(jax.ShapeDtypeStruct(x.shape, F32), *c_shapes),
        grid=grid,
        in_specs=[
            pl.BlockSpec((1, tm, d), lambda b, i: (b, i, 0)),
            pl.BlockSpec((1, 1, 6 * d), lambda b, i: (b, 0, 0)),
            _const_spec(g.shape),
            _const_spec(w_in.shape),
            _const_spec(conv_w.shape),
            _const_spec((1, d)),
            _const_spec(w_out.shape),
            *c_in,
        ],
        out_specs=(pl.BlockSpec((1, tm, d), lambda b, i: (b, i, 0)), *c_out),
        scratch_shapes=[pltpu.VMEM((tm + 8, d), F32)],
        compiler_params=_params(2),
        name="conv_mixer",
    )(x, mod, g, w_in, conv_w, conv_b.reshape(1, d), w_out, *c_args)


def _kv_tail(xn, rows, mod_ref, g_ref, wa_ref, ga_ref, wk_ref, wvt_ref, tab_ref,
             kn_ref, vt_ref, kpe_ref):
    d = xn.shape[1]
    mod = mod_ref[0]
    h = xn * _mod_gain(g_ref[...], mod[:, d:2 * d]) + mod[:, 0:d]
    kva = _dot(h.astype(BF16), wa_ref[...])
    c_kv = _rms(kva[:, 0:KV_LORA], ga_ref[...]).astype(BF16)
    pe = kva[:, KV_LORA:KV_LORA + LANES]
    prod = pe * tab_ref[rows, :]
    lane = lax.broadcasted_iota(jnp.int32, prod.shape, 1)
    kpe = jnp.where(lane < QK_ROPE, prod + pltpu.roll(prod, QK_ROPE, 1), 0.0)
    kpe_ref[0, rows, :] = kpe.astype(BF16)
    kn = _dot(c_kv, wk_ref[...]).astype(BF16)
    for hd in range(N_HEADS):
        kn_ref[0, hd, rows, :] = kn[:, hd * QK_NOPE:(hd + 1) * QK_NOPE]
    vt_ref[0, :, rows] = _dot_nt(wvt_ref[...], c_kv).astype(BF16)


def _q_tail(xn, rows, mod_ref, g_ref, wa_ref, ga_ref, wbt_ref, tab_ref, q_ref):
    d = xn.shape[1]
    n_rows = xn.shape[0]
    hw = 2 * LANES
    mod = mod_ref[0]
    h = xn * _mod_gain(g_ref[0:1, :], mod[:, d:2 * d]) + mod[:, 0:d]
    qa = _rms(_dot(h.astype(BF16), wa_ref[...]), ga_ref[...]).astype(BF16)
    half = N_HEADS * hw // 2
    qt = [_dot_nt(wbt_ref[0:half, :], qa), _dot_nt(wbt_ref[half:2 * half, :], qa)]
    tab = tab_ref[:, rows]
    cos_t, sin_t = tab[0:QK_ROPE, :], tab[QK_ROPE:2 * QK_ROPE, :]
    zeros = jnp.zeros((hw - QK_NOPE - QK_ROPE, n_rows), BF16)
    for hd in range(N_HEADS):
        part, r0 = qt[hd * hw // half], (hd * hw) % half
        blk = part[r0:r0 + hw, :]
        rope = (blk[QK_NOPE:QK_NOPE + QK_ROPE, :] * cos_t
                + blk[QK_NOPE + QK_ROPE:hw, :] * sin_t)
        q_ref[0, hd, 0:QK_NOPE, rows] = (blk[0:QK_NOPE, :] * Q_SCALE).astype(BF16)
        q_ref[0, hd, QK_NOPE:QK_NOPE + QK_ROPE, rows] = rope.astype(BF16)
        q_ref[0, hd, QK_NOPE + QK_ROPE:hw, rows] = zeros


N_KV_IN, N_KV_OUT = 7, 3
N_Q_IN, N_Q_OUT = 6, 1


def _mlp_kernel(*refs, with_oproj, with_kv, with_q, n_casts):
    refs = list(refs)
    x_ref, mod_ref, g_ref, up_ref, down_ref = refs[:5]
    pos = 5
    if with_oproj:
        a_ref, ow_ref = refs[pos:pos + 2]
        pos += 2
    if with_kv:
        kv_in = refs[pos:pos + N_KV_IN]
        pos += N_KV_IN
    if with_q:
        q_in = refs[pos:pos + N_Q_IN]
        pos += N_Q_IN
    cast_src = refs[pos:pos + n_casts]
    pos += n_casts
    o_ref = refs[pos]
    pos += 1
    if with_kv:
        kv_out = refs[pos:pos + N_KV_OUT]
        pos += N_KV_OUT
    if with_q:
        q_out = refs[pos:pos + N_Q_OUT]
        pos += N_Q_OUT
    cast_dst = refs[pos:pos + n_casts]
    _run_cast_jobs(cast_src, cast_dst)

    tm, d = x_ref.shape[1], x_ref.shape[2]
    d_ff = up_ref.shape[1]
    rs = ROW_SUB
    n_sub = tm // rs
    mod = mod_ref[0]
    gate_m = mod[:, 2 * d:3 * d]
    shift, gate = mod[:, 3 * d:4 * d], mod[:, 5 * d:6 * d]
    gain = _mod_gain(g_ref[2:3, :], mod[:, 4 * d:5 * d])
    chunks = list(range(0, d_ff, FF_CHUNK))

    def prologue(r):
        rows = slice(r * rs, (r + 1) * rs)
        x = x_ref[0, rows, :]
        if with_oproj:
            a = jnp.concatenate([a_ref[0, hd, rows, :] for hd in range(N_HEADS)],
                                axis=1)
            y = _dot(a, ow_ref[...])
            x = x + gate_m * _rms(y, g_ref[1:2, :])
        h = (_normalize(x) * gain + shift).astype(BF16)
        return x, h

    def ff_chunk(h, k, y):
        hid = _dot(h, up_ref[:, k:k + FF_CHUNK])
        act = jnp.square(jnp.maximum(hid, 0.0)).astype(BF16)
        part = _dot(act, down_ref[k:k + FF_CHUNK, :])
        return part if y is None else y + part

    def epilogue(r, x, y):
        rows = slice(r * rs, (r + 1) * rs)
        x_new = x + gate * _rms(y, g_ref[3:4, :])
        o_ref[0, rows, :] = x_new
        if with_kv or with_q:
            xn = _normalize(x_new)
        if with_kv:
            _kv_tail(xn, rows, *kv_in, *kv_out)
        if with_q:
            _q_tail(xn, rows, *q_in, *q_out)

    x, h = prologue(0)
    pending = None
    for r in range(n_sub):
        y = None
        for ci, k in enumerate(chunks):
            if ci == len(chunks) - 1 and r + 1 < n_sub:
                nxt = prologue(r + 1)
            y = ff_chunk(h, k, y)
            if ci == 0 and pending is not None:
                epilogue(*pending)
                pending = None
        pending = (r, x, y)
        if r + 1 < n_sub:
            x, h = nxt
    epilogue(*pending)


def _mlp_layer(x, mod, g, w_up, w_down, oproj=None, kv=None, q=None, casts=()):
    bsz, s, d = x.shape
    tm = TOKEN_TILE
    grid = (bsz, s // tm)
    c_args, c_in, c_out, c_shapes = _cast_jobs(casts, grid)
    row = lambda n: pl.BlockSpec((1, tm, n), lambda b, i: (b, i, 0))
    vec = lambda n: pl.BlockSpec((1, 1, n), lambda b, i: (b, 0, 0))
    table = lambda n: pl.BlockSpec((tm, n), lambda b, i: (i, 0))
    heads = lambda n: pl.BlockSpec((1, N_HEADS, tm, n), lambda b, i: (b, 0, i, 0))
    in_specs = [row(d), vec(6 * d), _const_spec(g.shape),
                _const_spec(w_up.shape), _const_spec(w_down.shape)]
    args = [x, mod, g, w_up, w_down]
    out_specs = [row(d)]
    out_shape = [jax.ShapeDtypeStruct(x.shape, F32)]
    name = "mlp"
    if oproj is not None:
        attn, w_o = oproj
        in_specs += [heads(attn.shape[3]), _const_spec(w_o.shape)]
        args += [attn, w_o]
        name = "oproj_" + name
    if kv is not None:
        assert len(kv) == N_KV_IN
        kv_mod, kv_g, w_a, g_a, w_k, w_vt, rope_tab = kv
        n_v = w_vt.shape[0]
        in_specs += [vec(2 * d), _const_spec((1, d)), _const_spec(w_a.shape),
                     _const_spec((1, KV_LORA)), _const_spec(w_k.shape),
                     _const_spec(w_vt.shape), table(LANES)]
        args += [kv_mod, kv_g.reshape(1, d), w_a, g_a.reshape(1, KV_LORA), w_k,
                 w_vt, rope_tab]
        out_specs += [heads(QK_NOPE),
                      pl.BlockSpec((1, n_v, tm), lambda b, i: (b, 0, i)),
                      row(LANES)]
        out_shape += [jax.ShapeDtypeStruct((bsz, N_HEADS, s, QK_NOPE), BF16),
                      jax.ShapeDtypeStruct((bsz, n_v, s), BF16),
                      jax.ShapeDtypeStruct((bsz, s, LANES), BF16)]
        name += "_kv"
    if q is not None:
        assert len(q) == N_Q_IN
        q_mod, q_g, w_a, g_a, w_b, rope_tab = q
        in_specs += [vec(6 * d), _const_spec(q_g.shape), _const_spec(w_a.shape),
                     _const_spec((1, Q_LORA)), _const_spec(w_b.shape),
                     pl.BlockSpec((2 * QK_ROPE, tm), lambda b, i: (0, i))]
        args += [q_mod, q_g, w_a, g_a.reshape(1, Q_LORA), w_b, rope_tab]
        out_specs += [pl.BlockSpec((1, N_HEADS, 2 * LANES, tm),
                                   lambda b, i: (b, 0, 0, i))]
        out_shape += [jax.ShapeDtypeStruct((bsz, N_HEADS, 2 * LANES, s), BF16)]
        name += "_q"
    return pl.pallas_call(
        functools.partial(_mlp_kernel, with_oproj=oproj is not None,
                          with_kv=kv is not None, with_q=q is not None,
                          n_casts=len(c_args)),
        out_shape=(*out_shape, *c_shapes),
        grid=grid,
        in_specs=in_specs + c_in,
        out_specs=(*out_specs, *c_out),
        compiler_params=_params(2),
        name=name,
    )(*args, *c_args)


def _col_reduce(x, op, final):
    n = x.shape[0]
    while n > 8 and n % 16 == 0:
        n //= 2
        x = op(x[:n], x[n:])
    return final(x, axis=0, keepdims=True)


def _attn_kernel(q_ref, kn_ref, kpe_ref, vt_ref, o_ref, kc_ref, s_ref, p_ref):
    s = q_ref.shape[3]
    tq = Q_TILE
    for hh in range(HEADS_PER_STEP):
        kc_ref[hh, :, 0:LANES] = kn_ref[0, hh]
        kc_ref[hh, :, LANES:2 * LANES] = kpe_ref[0]
    k_chunk = lax.broadcasted_iota(jnp.int32, (tq, tq), 0) // CHUNK
    q_chunk = lax.broadcasted_iota(jnp.int32, (tq, tq), 1) // CHUNK
    diag_mask = k_chunk <= q_chunk
    neg = jnp.finfo(F32).min
    n_tiles = s // tq

    def scores(item, slot):
        hh, i = item
        lo, hi = i * tq, (i + 1) * tq
        q = q_ref[0, hh, :, lo:hi]
        s_d = jnp.where(diag_mask, _dot(kc_ref[hh, lo:hi, :], q), neg)
        s_ref[slot, lo:hi, :] = s_d
        m = _col_reduce(s_d, jnp.maximum, jnp.max)
        for a in range(0, lo, tq):
            s_o = _dot(kc_ref[hh, a:a + tq, :], q)
            s_ref[slot, a:a + tq, :] = s_o
            m = jnp.maximum(m, _col_reduce(s_o, jnp.maximum, jnp.max))
        return m

    def softmax_pv(item, slot, m):
        hh, i = item
        lo, hi = i * tq, (i + 1) * tq
        blocks = [(j * tq, (j + 1) * tq) for j in range(i + 1)]
        l = None
        for a, b in blocks:
            p = jnp.exp2(s_ref[slot, a:b, :] - m)
            lb = _col_reduce(p, jnp.add, jnp.sum)
            l = lb if l is None else l + lb
            p_ref[slot, a:b, :] = p.astype(BF16)
        acc = _dot(vt_ref[0, hh * V_HEAD:(hh + 1) * V_HEAD, 0:hi],
                   p_ref[slot, 0:hi, :])
        o_ref[0, hh, lo:hi, :] = (acc * (1.0 / l)).T.astype(BF16)

    order = list(range(1, n_tiles, 2)) + list(range(n_tiles - 2 + n_tiles % 2, -1, -2))
    items = [(hh, i) for hh in range(HEADS_PER_STEP) for i in order]
    m_next = scores(items[0], 0)
    for pos, item in enumerate(items):
        m = m_next
        if pos + 1 < len(items):
            m_next = scores(items[pos + 1], (pos + 1) % 2)
        softmax_pv(item, pos % 2, m)


def _attention(q_cat, k_nope, k_pe, v_t):
    bsz, _, _, s = q_cat.shape
    hps = HEADS_PER_STEP
    assert N_HEADS % hps == 0
    head = lambda n: pl.BlockSpec((1, hps, s, n), lambda b, h: (b, h, 0, 0))
    return pl.pallas_call(
        _attn_kernel,
        out_shape=jax.ShapeDtypeStruct((bsz, N_HEADS, s, V_HEAD), BF16),
        grid=(bsz, N_HEADS // hps),
        in_specs=[
            pl.BlockSpec((1, hps, 2 * LANES, s), lambda b, h: (b, h, 0, 0)),
            head(QK_NOPE),
            pl.BlockSpec((1, s, LANES), lambda b, h: (b, 0, 0)),
            pl.BlockSpec((1, hps * V_HEAD, s), lambda b, h: (b, h, 0)),
        ],
        out_specs=head(V_HEAD),
        scratch_shapes=[pltpu.VMEM((hps, s, 2 * LANES), BF16),
                        pltpu.VMEM((2, s, Q_TILE), F32),
                        pltpu.VMEM((2, s, Q_TILE), BF16)],
        compiler_params=_params(2),
        name="mla_attention",
    )(q_cat, k_nope, k_pe, v_t)


def _half_swap(w):
    half = w.shape[-1] // 2
    return jnp.concatenate([w[..., half:], w[..., :half]], axis=-1)


def _rope_tables(seq_len):
    inv_freq = 1.0 / (ROPE_THETA ** (jnp.arange(0, QK_ROPE, 2, dtype=F32) / QK_ROPE))
    ang_t = inv_freq[:, None] * jnp.arange(seq_len, dtype=F32)[None, :]
    cos_t, sin_t = jnp.cos(ang_t), jnp.sin(ang_t)
    tab_t = jnp.concatenate([cos_t, cos_t, -sin_t, sin_t], axis=0)
    return tab_t.T, tab_t * Q_SCALE


def kernel(x, c, ada_w, ada_b, norm_g, conv_in_w, conv_w, conv_b, conv_out_w,
           kv_ada_w, kv_ada_b, kv_norm_g, kv_a_w, kv_a_norm_g, kv_b_w,
           q_a_w, q_a_norm_g, q_b_w, attn_o_w, mlp_up_w, mlp_down_w):
    bsz, s, d = x.shape
    depth = ada_w.shape[0]
    n_a = conv_in_w.shape[0]
    assert 1 <= n_a < depth, "needs a conv layer before the first attention layer"

    mod = _ada(c, ada_w, ada_b).reshape(depth, bsz, 1, 6 * d)
    kv_mod = _ada(c, kv_ada_w[None], kv_ada_b[None]).reshape(bsz, 1, 2 * d)
    k_tab, q_tab = _rope_tables(s)

    w_pe = kv_a_w[:, KV_LORA:]
    w_kva = jnp.concatenate([kv_a_w[:, :KV_LORA], w_pe, _half_swap(w_pe)], axis=1)
    w_kvb = kv_b_w.reshape(KV_LORA, N_HEADS, QK_NOPE + V_HEAD)
    w_k = w_kvb[:, :, :QK_NOPE].reshape(KV_LORA, -1)
    w_vt = w_kvb[:, :, QK_NOPE:].reshape(KV_LORA, -1).T
    kv_args = (kv_mod, kv_norm_g, w_kva.astype(BF16), kv_a_norm_g,
               w_k.astype(BF16), w_vt.astype(BF16), k_tab)

    def q_args(l):
        j = l - n_a
        w_qb = q_b_w[j].reshape(Q_LORA, N_HEADS, QK_NOPE + QK_ROPE)
        w_qb = jnp.concatenate([w_qb, _half_swap(w_qb[:, :, QK_NOPE:])], axis=-1)
        w_qbt = w_qb.reshape(Q_LORA, N_HEADS * 2 * LANES).T
        return (mod[l], norm_g[l], q_a_w[j].astype(BF16), q_a_norm_g[j],
                w_qbt.astype(BF16), q_tab)

    def mixer_weights(l):
        if l < n_a:
            return [(conv_in_w, l), (conv_out_w, l)]
        return [(attn_o_w, l - n_a)]

    mix_w = [w[l].astype(BF16) for w, l in mixer_weights(0)]
    k_nope = v_t = k_pe = q_cat = None
    for l in range(depth):
        mlp_casts = [(mlp_up_w, l), (mlp_down_w, l)]
        next_casts = mixer_weights(l + 1) if l + 1 < depth else []
        oproj = None
        if l < n_a:
            x, w_up, w_down = _conv_layer(x, mod[l], norm_g[l], mix_w[0], conv_w[l],
                                          conv_b[l], mix_w[1], mlp_casts)
        else:
            attn = _attention(q_cat, k_nope, k_pe, v_t)
            oproj = (attn, mix_w[0])
            w_up, w_down = mlp_w
        if l + 1 >= n_a:
            next_casts = next_casts + ([(mlp_up_w, l + 1), (mlp_down_w, l + 1)]
                                       if l + 1 < depth else [])
        kv = kv_args if l + 1 == n_a else None
        q = q_args(l + 1) if n_a <= l + 1 < depth else None
        outs = _mlp_layer(x, mod[l], norm_g[l], w_up, w_down, oproj, kv, q,
                          next_casts)
        x = outs[0]
        if kv is not None:
            k_nope, v_t, k_pe = outs[1:4]
        if q is not None:
            q_cat = outs[4] if kv is not None else outs[1]
        cast_outs = list(outs[len(outs) - len(next_casts):]) if next_casts else []
        n_mix = len(mixer_weights(l + 1)) if l + 1 < depth else 0
        mix_w, mlp_w = cast_outs[:n_mix], cast_outs[n_mix:]
    return x
```

```python
import functools
import math

import jax
import jax.numpy as jnp
from jax import lax
from jax.experimental import pallas as pl
from jax.experimental.pallas import tpu as pltpu

CHUNK = 64
CONV_W = 3
N_HEADS = 8
QK_NOPE = 128
QK_ROPE = 64
V_HEAD = 128
Q_LORA = 384
KV_LORA = 256
ROPE_THETA = 10000.0
NORM_EPS = 1e-6

V7X_VMEM_LIMIT_BYTES = 56 * 1024 * 1024
LANES = 128

TOKEN_TILE = 512
CONV_TOKEN_TILE = 1024
FF_CHUNK = 2048
ADA_COL_TILE = 1536
Q_TILE = 256
ROW_SUB = 256
HEADS_PER_STEP = 4

F32 = jnp.float32
BF16 = jnp.bfloat16

Q_SCALE = math.log2(math.e) / math.sqrt(QK_NOPE + QK_ROPE)


def _normalize(x):
    ms = jnp.mean(x * x, axis=-1, keepdims=True)
    return x * lax.rsqrt(ms + NORM_EPS)


def _rms(x, g):
    return _normalize(x) * g


def _mod_gain(g, scale):
    return g * (1.0 + scale)


def _dot(a, b):
    return jnp.dot(a, b, preferred_element_type=F32)


def _dot_nt(a, b):
    return lax.dot_general(a, b, (((1,), (1,)), ((), ())),
                           preferred_element_type=F32)


def _const_spec(shape):
    nd = len(shape)
    return pl.BlockSpec(shape, lambda *_: (0,) * nd,
                        pipeline_mode=pl.Buffered(1))


def _params(n_axes):
    return pltpu.CompilerParams(
        dimension_semantics=("arbitrary",) * n_axes,
        vmem_limit_bytes=V7X_VMEM_LIMIT_BYTES)


def _cast_jobs(weights, grid):
    n_b, n_i = grid
    args, in_specs, out_specs, out_shapes = [], [], [], []
    for w, layer in weights:
        _, rows, cols = w.shape
        rb = rows // (n_b * n_i)
        assert rb * n_b * n_i == rows and rb % 16 == 0, (w.shape, grid)
        args.append(w)
        in_specs.append(pl.BlockSpec(
            (1, rb, cols), lambda b, i, layer=layer: (layer, b * n_i + i, 0)))
        out_specs.append(pl.BlockSpec((rb, cols), lambda b, i: (b * n_i + i, 0)))
        out_shapes.append(jax.ShapeDtypeStruct((rows, cols), BF16))
    return args, in_specs, out_specs, out_shapes


def _run_cast_jobs(src_refs, dst_refs):
    for src, dst in zip(src_refs, dst_refs):
        dst[...] = src[0].astype(BF16)


def _ada_kernel(c_ref, w_ref, b_ref, o_ref):
    c = c_ref[...]
    c_act = (c * jax.nn.sigmoid(c)).astype(BF16)
    o_ref[0] = _dot(c_act, w_ref[0].astype(BF16)) + b_ref[0]


def _ada(c, w, b):
    n_layers, d, n = w.shape
    bsz = c.shape[0]
    tn = max(t for t in range(LANES, ADA_COL_TILE + 1, LANES) if n % t == 0)
    return pl.pallas_call(
        _ada_kernel,
        out_shape=jax.ShapeDtypeStruct((n_layers, bsz, n), F32),
        grid=(n_layers, n // tn),
        in_specs=[
            _const_spec((bsz, d)),
            pl.BlockSpec((1, d, tn), lambda l, j: (l, 0, j)),
            pl.BlockSpec((1, 1, tn), lambda l, j: (l, 0, j)),
        ],
        out_specs=pl.BlockSpec((1, bsz, tn), lambda l, j: (l, 0, j)),
        compiler_params=_params(2),
        name="ada_mod",
    )(c, w, b.reshape(n_layers, 1, n))


def _conv_kernel(*refs, n_casts):
    x_ref, mod_ref, g_ref, win_ref, cw_ref, cb_ref, wout_ref = refs[:7]
    cast_src = refs[7:7 + n_casts]
    o_ref = refs[7 + n_casts]
    cast_dst = refs[8 + n_casts:8 + 2 * n_casts]
    u_ref = refs[8 + 2 * n_casts]
    _run_cast_jobs(cast_src, cast_dst)
    tm, d = x_ref.shape[1], x_ref.shape[2]
    mod = mod_ref[0]
    shift, gate = mod[:, 0:d], mod[:, 2 * d:3 * d]
    gain = _mod_gain(g_ref[0:1, :], mod[:, d:2 * d])

    @pl.when(pl.program_id(1) == 0)
    def _():
        u_ref[0:8, :] = jnp.zeros((8, d), F32)

    @pl.when(pl.program_id(1) > 0)
    def _():
        u_ref[0:8, :] = u_ref[tm:tm + 8, :]

    x = x_ref[0]
    hb = (_normalize(x) * gain + shift).astype(BF16)
    cx = _dot(hb, win_ref[:, d:3 * d])
    bg = _dot(hb, win_ref[:, 0:d])
    u = cx[:, 0:d] * cx[:, d:2 * d]
    u_ref[8:8 + tm, :] = u
    conv = (cw_ref[0:1, :] * u_ref[6:6 + tm, :]
            + cw_ref[1:2, :] * u_ref[7:7 + tm, :]
            + cw_ref[2:3, :] * u
            + cb_ref[...])
    y = _dot((bg * conv).astype(BF16), wout_ref[...])
    o_ref[0] = x + gate * _rms(y, g_ref[1:2, :])


def _conv_layer(x, mod, g, w_in, conv_w, conv_b, w_out, casts=()):
    bsz, s, d = x.shape
    tm = CONV_TOKEN_TILE
    grid = (bsz, s // tm)
    c_args, c_in, c_out, c_shapes = _cast_jobs(casts, grid)
    return pl.pallas_call(
        functools.partial(_conv_kernel, n_casts=len(c_args)),
        out_shape=(jax.ShapeDtypeStruct(x.shape, F32), *c_shapes),
        grid=grid,
        in_specs=[
            pl.BlockSpec((1, tm, d), lambda b, i: (b, i, 0)),
            pl.BlockSpec((1, 1, 6 * d), lambda b, i: (b, 0, 0)),
            _const_spec(g.shape),
            _const_spec(w_in.shape),
            _const_spec(conv_w.shape),
            _const_spec((1, d)),
            _const_spec(w_out.shape),
            *c_in,
        ],
        out_specs=(pl.BlockSpec((1, tm, d), lambda b, i: (b, i, 0)), *c_out),
        scratch_shapes=[pltpu.VMEM((tm + 8, d), F32)],
        compiler_params=_params(2),
        name="conv_mixer",
    )(x, mod, g, w_in, conv_w, conv_b.reshape(1, d), w_out, *c_args)


def _kv_tail(xn, rows, mod_ref, g_ref, wa_ref, ga_ref, wk_ref, wvt_ref, tab_ref,
             kn_ref, vt_ref, kpe_ref):
    d = xn.shape[1]
    mod = mod_ref[0]
    h = xn * _mod_gain(g_ref[...], mod[:, d:2 * d]) + mod[:, 0:d]
    kva = _dot(h.astype(BF16), wa_ref[...])
    c_kv = _rms(kva[:, 0:KV_LORA], ga_ref[...]).astype(BF16)
    pe = kva[:, KV_LORA:KV_LORA + LANES]
    prod = pe * tab_ref[rows, :]
    lane = lax.broadcasted_iota(jnp.int32, prod.shape, 1)
    kpe = jnp.where(lane < QK_ROPE, prod + pltpu.roll(prod, QK_ROPE, 1), 0.0)
    kpe_ref[0, rows, :] = kpe.astype(BF16)
    kn = _dot(c_kv, wk_ref[...]).astype(BF16)
    for hd in range(N_HEADS):
        kn_ref[0, hd, rows, :] = kn[:, hd * QK_NOPE:(hd + 1) * QK_NOPE]
    vt_ref[0, :, rows] = _dot_nt(wvt_ref[...], c_kv).astype(BF16)


def _q_tail(xn, rows, mod_ref, g_ref, wa_ref, ga_ref, wbt_ref, tab_ref, q_ref):
    d = xn.shape[1]
    n_rows = xn.shape[0]
    hw = 2 * LANES
    mod = mod_ref[0]
    h = xn * _mod_gain(g_ref[0:1, :], mod[:, d:2 * d]) + mod[:, 0:d]
    qa = _rms(_dot(h.astype(BF16), wa_ref[...]), ga_ref[...]).astype(BF16)
    half = N_HEADS * hw // 2
    qt = [_dot_nt(wbt_ref[0:half, :], qa), _dot_nt(wbt_ref[half:2 * half, :], qa)]
    tab = tab_ref[:, rows]
    cos_t, sin_t = tab[0:QK_ROPE, :], tab[QK_ROPE:2 * QK_ROPE, :]
    zeros = jnp.zeros((hw - QK_NOPE - QK_ROPE, n_rows), BF16)
    for hd in range(N_HEADS):
        part, r0 = qt[hd * hw // half], (hd * hw) % half
        blk = part[r0:r0 + hw, :]
        rope = (blk[QK_NOPE:QK_NOPE + QK_ROPE, :] * cos_t
                + blk[QK_NOPE + QK_ROPE:hw, :] * sin_t)
        q_ref[0, hd, 0:QK_NOPE, rows] = (blk[0:QK_NOPE, :] * Q_SCALE).astype(BF16)
        q_ref[0, hd, QK_NOPE:QK_NOPE + QK_ROPE, rows] = rope.astype(BF16)
        q_ref[0, hd, QK_NOPE + QK_ROPE:hw, rows] = zeros


N_KV_IN, N_KV_OUT = 7, 3
N_Q_IN, N_Q_OUT = 6, 1


def _mlp_kernel(*refs, with_oproj, with_kv, with_q, n_casts):
    refs = list(refs)
    x_ref, mod_ref, g_ref, up_ref, down_ref = refs[:5]
    pos = 5
    if with_oproj:
        a_ref, ow_ref = refs[pos:pos + 2]
        pos += 2
    if with_kv:
        kv_in = refs[pos:pos + N_KV_IN]
        pos += N_KV_IN
    if with_q:
        q_in = refs[pos:pos + N_Q_IN]
        pos += N_Q_IN
    cast_src = refs[pos:pos + n_casts]
    pos += n_casts
    o_ref = refs[pos]
    pos += 1
    if with_kv:
        kv_out = refs[pos:pos + N_KV_OUT]
        pos += N_KV_OUT
    if with_q:
        q_out = refs[pos:pos + N_Q_OUT]
        pos += N_Q_OUT
    cast_dst = refs[pos:pos + n_casts]
    _run_cast_jobs(cast_src, cast_dst)

    tm, d = x_ref.shape[1], x_ref.shape[2]
    d_ff = up_ref.shape[1]
    rs = ROW_SUB
    n_sub = tm // rs
    mod = mod_ref[0]
    gate_m = mod[:, 2 * d:3 * d]
    shift, gate = mod[:, 3 * d:4 * d], mod[:, 5 * d:6 * d]
    gain = _mod_gain(g_ref[2:3, :], mod[:, 4 * d:5 * d])
    chunks = list(range(0, d_ff, FF_CHUNK))

    def prologue(r):
        rows = slice(r * rs, (r + 1) * rs)
        x = x_ref[0, rows, :]
        if with_oproj:
            a = jnp.concatenate([a_ref[0, hd, rows, :] for hd in range(N_HEADS)],
                                axis=1)
            y = _dot(a, ow_ref[...])
            x = x + gate_m * _rms(y, g_ref[1:2, :])
        h = (_normalize(x) * gain + shift).astype(BF16)
        return x, h

    def ff_chunk(h, k, y):
        hid = _dot(h, up_ref[:, k:k + FF_CHUNK])
        act = jnp.square(jnp.maximum(hid, 0.0)).astype(BF16)
        part = _dot(act, down_ref[k:k + FF_CHUNK, :])
        return part if y is None else y + part

    def epilogue(r, x, y):
        rows = slice(r * rs, (r + 1) * rs)
        x_new = x + gate * _rms(y, g_ref[3:4, :])
        o_ref[0, rows, :] = x_new
        if with_kv or with_q:
            xn = _normalize(x_new)
        if with_kv:
            _kv_tail(xn, rows, *kv_in, *kv_out)
        if with_q:
            _q_tail(xn, rows, *q_in, *q_out)

    x, h = prologue(0)
    pending = None
    for r in range(n_sub):
        y = None
        for ci, k in enumerate(chunks):
            if ci == len(chunks) - 1 and r + 1 < n_sub:
                nxt = prologue(r + 1)
            y = ff_chunk(h, k, y)
            if ci == 0 and pending is not None:
                epilogue(*pending)
                pending = None
        pending = (r, x, y)
        if r + 1 < n_sub:
            x, h = nxt
    epilogue(*pending)


def _mlp_layer(x, mod, g, w_up, w_down, oproj=None, kv=None, q=None, casts=()):
    bsz, s, d = x.shape
    tm = TOKEN_TILE
    grid = (bsz, s // tm)
    c_args, c_in, c_out, c_shapes = _cast_jobs(casts, grid)
    row = lambda n: pl.BlockSpec((1, tm, n), lambda b, i: (b, i, 0))
    vec = lambda n: pl.BlockSpec((1, 1, n), lambda b, i: (b, 0, 0))
    table = lambda n: pl.BlockSpec((tm, n), lambda b, i: (i, 0))
    heads = lambda n: pl.BlockSpec((1, N_HEADS, tm, n), lambda b, i: (b, 0, i, 0))
    in_specs = [row(d), vec(6 * d), _const_spec(g.shape),
                _const_spec(w_up.shape), _const_spec(w_down.shape)]
    args = [x, mod, g, w_up, w_down]
    out_specs = [row(d)]
    out_shape = [jax.ShapeDtypeStruct(x.shape, F32)]
    name = "mlp"
    if oproj is not None:
        attn, w_o = oproj
        in_specs += [heads(attn.shape[3]), _const_spec(w_o.shape)]
        args += [attn, w_o]
        name = "oproj_" + name
    if kv is not None:
        assert len(kv) == N_KV_IN
        kv_mod, kv_g, w_a, g_a, w_k, w_vt, rope_tab = kv
        n_v = w_vt.shape[0]
        in_specs += [vec(2 * d), _const_spec((1, d)), _const_spec(w_a.shape),
                     _const_spec((1, KV_LORA)), _const_spec(w_k.shape),
                     _const_spec(w_vt.shape), table(LANES)]
        args += [kv_mod, kv_g.reshape(1, d), w_a, g_a.reshape(1, KV_LORA), w_k,
                 w_vt, rope_tab]
        out_specs += [heads(QK_NOPE),
                      pl.BlockSpec((1, n_v, tm), lambda b, i: (b, 0, i)),
                      row(LANES)]
        out_shape += [jax.ShapeDtypeStruct((bsz, N_HEADS, s, QK_NOPE), BF16),
                      jax.ShapeDtypeStruct((bsz, n_v, s), BF16),
                      jax.ShapeDtypeStruct((bsz, s, LANES), BF16)]
        name += "_kv"
    if q is not None:
        assert len(q) == N_Q_IN
        q_mod, q_g, w_a, g_a, w_b, rope_tab = q
        in_specs += [vec(6 * d), _const_spec(q_g.shape), _const_spec(w_a.shape),
                     _const_spec((1, Q_LORA)), _const_spec(w_b.shape),
                     pl.BlockSpec((2 * QK_ROPE, tm), lambda b, i: (0, i))]
        args += [q_mod, q_g, w_a, g_a.reshape(1, Q_LORA), w_b, rope_tab]
        out_specs += [pl.BlockSpec((1, N_HEADS, 2 * LANES, tm),
                                   lambda b, i: (b, 0, 0, i))]
        out_shape += [jax.ShapeDtypeStruct((bsz, N_HEADS, 2 * LANES, s), BF16)]
        name += "_q"
    return pl.pallas_call(
        functools.partial(_mlp_kernel, with_oproj=oproj is not None,
                          with_kv=kv is not None, with_q=q is not None,
                          n_casts=len(c_args)),
        out_shape=(*out_shape, *c_shapes),
        grid=grid,
        in_specs=in_specs + c_in,
        out_specs=(*out_specs, *c_out),
        compiler_params=_params(2),
        name=name,
    )(*args, *c_args)


def _col_reduce(x, op, final):
    n = x.shape[0]
    while n > 8 and n % 16 == 0:
        n //= 2
        x = op(x[:n], x[n:])
    return final(x, axis=0, keepdims=True)


def _attn_kernel(q_ref, kn_ref, kpe_ref, vt_ref, o_ref, kc_ref, s_ref, p_ref):
    s = q_ref.shape[3]
    tq = Q_TILE
    for hh in range(HEADS_PER_STEP):
        kc_ref[hh, :, 0:LANES] = kn_ref[0, hh]
        kc_ref[hh, :, LANES:2 * LANES] = kpe_ref[0]
    k_chunk = lax.broadcasted_iota(jnp.int32, (tq, tq), 0) // CHUNK
    q_chunk = lax.broadcasted_iota(jnp.int32, (tq, tq), 1) // CHUNK
    diag_mask = k_chunk <= q_chunk
    neg = jnp.finfo(F32).min
    n_tiles = s // tq

    def scores(item, slot):
        hh, i = item
        lo, hi = i * tq, (i + 1) * tq
        q = q_ref[0, hh, :, lo:hi]
        s_d = jnp.where(diag_mask, _dot(kc_ref[hh, lo:hi, :], q), neg)
        s_ref[slot, lo:hi, :] = s_d
        m = _col_reduce(s_d, jnp.maximum, jnp.max)
        for a in range(0, lo, tq):
            s_o = _dot(kc_ref[hh, a:a + tq, :], q)
            s_ref[slot, a:a + tq, :] = s_o
            m = jnp.maximum(m, _col_reduce(s_o, jnp.maximum, jnp.max))
        return m

    def softmax_pv(item, slot, m):
        hh, i = item
        lo, hi = i * tq, (i + 1) * tq
        blocks = [(j * tq, (j + 1) * tq) for j in range(i + 1)]
        l = None
        for a, b in blocks:
            p = jnp.exp2(s_ref[slot, a:b, :] - m)
            lb = _col_reduce(p, jnp.add, jnp.sum)
            l = lb if l is None else l + lb
            p_ref[slot, a:b, :] = p.astype(BF16)
        acc = _dot(vt_ref[0, hh * V_HEAD:(hh + 1) * V_HEAD, 0:hi],
                   p_ref[slot, 0:hi, :])
        o_ref[0, hh, lo:hi, :] = (acc * (1.0 / l)).T.astype(BF16)

    order = list(range(1, n_tiles, 2)) + list(range(n_tiles - 2 + n_tiles % 2, -1, -2))
    items = [(hh, i) for hh in range(HEADS_PER_STEP) for i in order]
    m_next = scores(items[0], 0)
    for pos, item in enumerate(items):
        m = m_next
        if pos + 1 < len(items):
            m_next = scores(items[pos + 1], (pos + 1) % 2)
        softmax_pv(item, pos % 2, m)


def _attention(q_cat, k_nope, k_pe, v_t):
    bsz, _, _, s = q_cat.shape
    hps = HEADS_PER_STEP
    assert N_HEADS % hps == 0
    head = lambda n: pl.BlockSpec((1, hps, s, n), lambda b, h: (b, h, 0, 0))
    return pl.pallas_call(
        _attn_kernel,
        out_shape=jax.ShapeDtypeStruct((bsz, N_HEADS, s, V_HEAD), BF16),
        grid=(bsz, N_HEADS // hps),
        in_specs=[
            pl.BlockSpec((1, hps, 2 * LANES, s), lambda b, h: (b, h, 0, 0)),
            head(QK_NOPE),
            pl.BlockSpec((1, s, LANES), lambda b, h: (b, 0, 0)),
            pl.BlockSpec((1, hps * V_HEAD, s), lambda b, h: (b, h, 0)),
        ],
        out_specs=head(V_HEAD),
        scratch_shapes=[pltpu.VMEM((hps, s, 2 * LANES), BF16),
                        pltpu.VMEM((2, s, Q_TILE), F32),
                        pltpu.VMEM((2, s, Q_TILE), BF16)],
        compiler_params=_params(2),
        name="mla_attention",
    )(q_cat, k_nope, k_pe, v_t)


def _half_swap(w):
    half = w.shape[-1] // 2
    return jnp.concatenate([w[..., half:], w[..., :half]], axis=-1)


def _rope_tables(seq_len):
    inv_freq = 1.0 / (ROPE_THETA ** (jnp.arange(0, QK_ROPE, 2, dtype=F32) / QK_ROPE))
    ang_t = inv_freq[:, None] * jnp.arange(seq_len, dtype=F32)[None, :]
    cos_t, sin_t = jnp.cos(ang_t), jnp.sin(ang_t)
    tab_t = jnp.concatenate([cos_t, cos_t, -sin_t, sin_t], axis=0)
    return tab_t.T, tab_t * Q_SCALE


def kernel(x, c, ada_w, ada_b, norm_g, conv_in_w, conv_w, conv_b, conv_out_w,
           kv_ada_w, kv_ada_b, kv_norm_g, kv_a_w, kv_a_norm_g, kv_b_w,
           q_a_w, q_a_norm_g, q_b_w, attn_o_w, mlp_up_w, mlp_down_w):
    bsz, s, d = x.shape
    depth = ada_w.shape[0]
    n_a = conv_in_w.shape[0]
    assert 1 <= n_a < depth, "needs a conv layer before the first attention layer"

    mod = _ada(c, ada_w, ada_b).reshape(depth, bsz, 1, 6 * d)
    kv_mod = _ada(c, kv_ada_w[None], kv_ada_b[None]).reshape(bsz, 1, 2 * d)
    k_tab, q_tab = _rope_tables(s)

    w_pe = kv_a_w[:, KV_LORA:]
    w_kva = jnp.concatenate([kv_a_w[:, :KV_LORA], w_pe, _half_swap(w_pe)], axis=1)
    w_kvb = kv_b_w.reshape(KV_LORA, N_HEADS, QK_NOPE + V_HEAD)
    w_k = w_kvb[:, :, :QK_NOPE].reshape(KV_LORA, -1)
    w_vt = w_kvb[:, :, QK_NOPE:].reshape(KV_LORA, -1).T
    kv_args = (kv_mod, kv_norm_g, w_kva.astype(BF16), kv_a_norm_g,
               w_k.astype(BF16), w_vt.astype(BF16), k_tab)

    def q_args(l):
        j = l - n_a
        w_qb = q_b_w[j].reshape(Q_LORA, N_HEADS, QK_NOPE + QK_ROPE)
        w_qb = jnp.concatenate([w_qb, _half_swap(w_qb[:, :, QK_NOPE:])], axis=-1)
        w_qbt = w_qb.reshape(Q_LORA, N_HEADS * 2 * LANES).T
        return (mod[l], norm_g[l], q_a_w[j].astype(BF16), q_a_norm_g[j],
                w_qbt.astype(BF16), q_tab)

    def mixer_weights(l):
        if l < n_a:
            return [(conv_in_w, l), (conv_out_w, l)]
        return [(attn_o_w, l - n_a)]

    mix_w = [w[l].astype(BF16) for w, l in mixer_weights(0)]
    k_nope = v_t = k_pe = q_cat = None
    for l in range(depth):
        mlp_casts = [(mlp_up_w, l), (mlp_down_w, l)]
        next_casts = mixer_weights(l + 1) if l + 1 < depth else []
        oproj = None
        if l < n_a:
            x, w_up, w_down = _conv_layer(x, mod[l], norm_g[l], mix_w[0], conv_w[l],
                                          conv_b[l], mix_w[1], mlp_casts)
        else:
            attn = _attention(q_cat, k_nope, k_pe, v_t)
            oproj = (attn, mix_w[0])
            w_up, w_down = mlp_w
        if l + 1 >= n_a:
            next_casts = next_casts + ([(mlp_up_w, l + 1), (mlp_down_w, l + 1)]
                                       if l + 1 < depth else [])
        kv = kv_args if l + 1 == n_a else None
        q = q_args(l + 1) if n_a <= l + 1 < depth else None
        outs = _mlp_layer(x, mod[l], norm_g[l], w_up, w_down, oproj, kv, q,
                          next_casts)
        x = outs[0]
        if kv is not None:
            k_nope, v_t, k_pe = outs[1:4]
        if q is not None:
            q_cat = outs[4] if kv is not None else outs[1]
        cast_outs = list(outs[len(outs) - len(next_casts):]) if next_casts else []
        n_mix = len(mixer_weights(l + 1)) if l + 1 < depth else 0
        mix_w, mlp_w = cast_outs[:n_mix], cast_outs[n_mix:]
    return x
```

```python
import functools
import math

import jax
import jax.numpy as jnp
from jax import lax
from jax.experimental import pallas as pl
from jax.experimental.pallas import tpu as pltpu

CHUNK = 64
CONV_W = 3
N_HEADS = 8
QK_NOPE = 128
QK_ROPE = 64
V_HEAD = 128
Q_LORA = 384
KV_LORA = 256
ROPE_THETA = 10000.0
NORM_EPS = 1e-6

V7X_VMEM_LIMIT_BYTES = 56 * 1024 * 1024
LANES = 128

TOKEN_TILE = 512
CONV_TOKEN_TILE = 1024
FF_CHUNK = 2048
ADA_COL_TILE = 1536
Q_TILE = 256
ROW_SUB = 256
HEADS_PER_STEP = 4

F32 = jnp.float32
BF16 = jnp.bfloat16

Q_SCALE = math.log2(math.e) / math.sqrt(QK_NOPE + QK_ROPE)


def _normalize(x):
    ms = jnp.mean(x * x, axis=-1, keepdims=True)
    return x * lax.rsqrt(ms + NORM_EPS)


def _rms(x, g):
    return _normalize(x) * g


def _mod_gain(g, scale):
    return g * (1.0 + scale)


def _dot(a, b):
    return jnp.dot(a, b, preferred_element_type=F32)


def _dot_nt(a, b):
    return lax.dot_general(a, b, (((1,), (1,)), ((), ())),
                           preferred_element_type=F32)


def _const_spec(shape):
    nd = len(shape)
    return pl.BlockSpec(shape, lambda *_: (0,) * nd,
                        pipeline_mode=pl.Buffered(1))


def _params(n_axes):
    return pltpu.CompilerParams(
        dimension_semantics=("arbitrary",) * n_axes,
        vmem_limit_bytes=V7X_VMEM_LIMIT_BYTES)


def _cast_jobs(weights, grid):
    n_b, n_i = grid
    args, in_specs, out_specs, out_shapes = [], [], [], []
    for w, layer in weights:
        _, rows, cols = w.shape
        rb = rows // (n_b * n_i)
        assert rb * n_b * n_i == rows and rb % 16 == 0, (w.shape, grid)
        args.append(w)
        in_specs.append(pl.BlockSpec(
            (1, rb, cols), lambda b, i, layer=layer: (layer, b * n_i + i, 0)))
        out_specs.append(pl.BlockSpec((rb, cols), lambda b, i: (b * n_i + i, 0)))
        out_shapes.append(jax.ShapeDtypeStruct((rows, cols), BF16))
    return args, in_specs, out_specs, out_shapes


def _run_cast_jobs(src_refs, dst_refs):
    for src, dst in zip(src_refs, dst_refs):
        dst[...] = src[0].astype(BF16)


def _ada_kernel(c_ref, w_ref, b_ref, o_ref):
    c = c_ref[...]
    c_act = (c * jax.nn.sigmoid(c)).astype(BF16)
    o_ref[0] = _dot(c_act, w_ref[0].astype(BF16)) + b_ref[0]


def _ada(c, w, b):
    n_layers, d, n = w.shape
    bsz = c.shape[0]
    tn = max(t for t in range(LANES, ADA_COL_TILE + 1, LANES) if n % t == 0)
    return pl.pallas_call(
        _ada_kernel,
        out_shape=jax.ShapeDtypeStruct((n_layers, bsz, n), F32),
        grid=(n_layers, n // tn),
        in_specs=[
            _const_spec((bsz, d)),
            pl.BlockSpec((1, d, tn), lambda l, j: (l, 0, j)),
            pl.BlockSpec((1, 1, tn), lambda l, j: (l, 0, j)),
        ],
        out_specs=pl.BlockSpec((1, bsz, tn), lambda l, j: (l, 0, j)),
        compiler_params=_params(2),
        name="ada_mod",
    )(c, w, b.reshape(n_layers, 1, n))


def _conv_kernel(*refs, n_casts):
    x_ref, mod_ref, g_ref, win_ref, cw_ref, cb_ref, wout_ref = refs[:7]
    cast_src = refs[7:7 + n_casts]
    o_ref = refs[7 + n_casts]
    cast_dst = refs[8 + n_casts:8 + 2 * n_casts]
    u_ref = refs[8 + 2 * n_casts]
    _run_cast_jobs(cast_src, cast_dst)
    tm, d = x_ref.shape[1], x_ref.shape[2]
    mod = mod_ref[0]
    shift, gate = mod[:, 0:d], mod[:, 2 * d:3 * d]
    gain = _mod_gain(g_ref[0:1, :], mod[:, d:2 * d])

    @pl.when(pl.program_id(1) == 0)
    def _():
        u_ref[0:8, :] = jnp.zeros((8, d), F32)

    @pl.when(pl.program_id(1) > 0)
    def _():
        u_ref[0:8, :] = u_ref[tm:tm + 8, :]

    x = x_ref[0]
    hb = (_normalize(x) * gain + shift).astype(BF16)
    cx = _dot(hb, win_ref[:, d:3 * d])
    bg = _dot(hb, win_ref[:, 0:d])
    u = cx[:, 0:d] * cx[:, d:2 * d]
    u_ref[8:8 + tm, :] = u
    conv = (cw_ref[0:1, :] * u_ref[6:6 + tm, :]
            + cw_ref[1:2, :] * u_ref[7:7 + tm, :]
            + cw_ref[2:3, :] * u
            + cb_ref[...])
    y = _dot((bg * conv).astype(BF16), wout_ref[...])
    o_ref[0] = x + gate * _rms(y, g_ref[1:2, :])


def _conv_layer(x, mod, g, w_in, conv_w, conv_b, w_out, casts=()):
    bsz, s, d = x.shape
    tm = CONV_TOKEN_TILE
    grid = (bsz, s // tm)
    c_args, c_in, c_out, c_shapes = _cast_jobs(casts, grid)
    return pl.pallas_call(
        functools.partial(_conv_kernel, n_casts=len(c_args)),
        out_shape=(jax.ShapeDtypeStruct(x.shape, F32), *c_shapes),
        grid=grid,
        in_specs=[
            pl.BlockSpec((1, tm, d), lambda b, i: (b, i, 0)),
            pl.BlockSpec((1, 1, 6 * d), lambda b, i: (b, 0, 0)),
            _const_spec(g.shape),
            _const_spec(w_in.shape),
            _const_spec(conv_w.shape),
            _const_spec((1, d)),
            _const_spec(w_out.shape),
            *c_in,
        ],
        out_specs=(pl.BlockSpec((1, tm, d), lambda b, i: (b, i, 0)), *c_out),
        scratch_shapes=[pltpu.VMEM((tm + 8, d), F32)],
        compiler_params=_params(2),
        name="conv_mixer",
    )(x, mod, g, w_in, conv_w, conv_b.reshape(1, d), w_out, *c_args)


def _kv_tail(xn, rows, mod_ref, g_ref, wa_ref, ga_ref, wk_ref, wvt_ref, tab_ref,
             kn_ref, vt_ref, kpe_ref):
    d = xn.shape[1]
    mod = mod_ref[0]
    h = xn * _mod_gain(g_ref[...], mod[:, d:2 * d]) + mod[:, 0:d]
    kva = _dot(h.astype(BF16), wa_ref[...])
    c_kv = _rms(kva[:, 0:KV_LORA], ga_ref[...]).astype(BF16)
    pe = kva[:, KV_LORA:KV_LORA + LANES]
    prod = pe * tab_ref[rows, :]
    lane = lax.broadcasted_iota(jnp.int32, prod.shape, 1)
    kpe = jnp.where(lane < QK_ROPE, prod + pltpu.roll(prod, QK_ROPE, 1), 0.0)
    kpe_ref[0, rows, :] = kpe.astype(BF16)
    kn = _dot(c_kv, wk_ref[...]).astype(BF16)
    for hd in range(N_HEADS):
        kn_ref[0, hd, rows, :] = kn[:, hd * QK_NOPE:(hd + 1) * QK_NOPE]
    vt_ref[0, :, rows] = _dot_nt(wvt_ref[...], c_kv).astype(BF16)


def _q_tail(xn, rows, mod_ref, g_ref, wa_ref, ga_ref, wbt_ref, tab_ref, q_ref):
    d = xn.shape[1]
    n_rows = xn.shape[0]
    hw = 2 * LANES
    mod = mod_ref[0]
    h = xn * _mod_gain(g_ref[0:1, :], mod[:, d:2 * d]) + mod[:, 0:d]
    qa = _rms(_dot(h.astype(BF16), wa_ref[...]), ga_ref[...]).astype(BF16)
    half = N_HEADS * hw // 2
    qt = [_dot_nt(wbt_ref[0:half, :], qa), _dot_nt(wbt_ref[half:2 * half, :], qa)]
    tab = tab_ref[:, rows]
    cos_t, sin_t = tab[0:QK_ROPE, :], tab[QK_ROPE:2 * QK_ROPE, :]
    zeros = jnp.zeros((hw - QK_NOPE - QK_ROPE, n_rows), BF16)
    for hd in range(N_HEADS):
        part, r0 = qt[hd * hw // half], (hd * hw) % half
        blk = part[r0:r0 + hw, :]
        rope = (blk[QK_NOPE:QK_NOPE + QK_ROPE, :] * cos_t
                + blk[QK_NOPE + QK_ROPE:hw, :] * sin_t)
        q_ref[0, hd, 0:QK_NOPE, rows] = (blk[0:QK_NOPE, :] * Q_SCALE).astype(BF16)
        q_ref[0, hd, QK_NOPE:QK_NOPE + QK_ROPE, rows] = rope.astype(BF16)
        q_ref[0, hd, QK_NOPE + QK_ROPE:hw, rows] = zeros


N_KV_IN, N_KV_OUT = 7, 3
N_Q_IN, N_Q_OUT = 6, 1


def _mlp_kernel(*refs, with_oproj, with_kv, with_q, n_casts):
    refs = list(refs)
    x_ref, mod_ref, g_ref, up_ref, down_ref = refs[:5]
    pos = 5
    if with_oproj:
        a_ref, ow_ref = refs[pos:pos + 2]
        pos += 2
    if with_kv:
        kv_in = refs[pos:pos + N_KV_IN]
        pos += N_KV_IN
    if with_q:
        q_in = refs[pos:pos + N_Q_IN]
        pos += N_Q_IN
    cast_src = refs[pos:pos + n_casts]
    pos += n_casts
    o_ref = refs[pos]
    pos += 1
    if with_kv:
        kv_out = refs[pos:pos + N_KV_OUT]
        pos += N_KV_OUT
    if with_q:
        q_out = refs[pos:pos + N_Q_OUT]
        pos += N_Q_OUT
    cast_dst = refs[pos:pos + n_casts]
    _run_cast_jobs(cast_src, cast_dst)

    tm, d = x_ref.shape[1], x_ref.shape[2]
    d_ff = up_ref.shape[1]
    rs = ROW_SUB
    n_sub = tm // rs
    mod = mod_ref[0]
    gate_m = mod[:, 2 * d:3 * d]
    shift, gate = mod[:, 3 * d:4 * d], mod[:, 5 * d:6 * d]
    gain = _mod_gain(g_ref[2:3, :], mod[:, 4 * d:5 * d])
    chunks = list(range(0, d_ff, FF_CHUNK))

    def prologue(r):
        rows = slice(r * rs, (r + 1) * rs)
        x = x_ref[0, rows, :]
        if with_oproj:
            a = jnp.concatenate([a_ref[0, hd, rows, :] for hd in range(N_HEADS)],
                                axis=1)
            y = _dot(a, ow_ref[...])
            x = x + gate_m * _rms(y, g_ref[1:2, :])
        h = (_normalize(x) * gain + shift).astype(BF16)
        return x, h

    def ff_chunk(h, k, y):
        hid = _dot(h, up_ref[:, k:k + FF_CHUNK])
        act = jnp.square(jnp.maximum(hid, 0.0)).astype(BF16)
        part = _dot(act, down_ref[k:k + FF_CHUNK, :])
        return part if y is None else y + part

    def epilogue(r, x, y):
        rows = slice(r * rs, (r + 1) * rs)
        x_new = x + gate * _rms(y, g_ref[3:4, :])
        o_ref[0, rows, :] = x_new
        if with_kv or with_q:
            xn = _normalize(x_new)
        if with_kv:
            _kv_tail(xn, rows, *kv_in, *kv_out)
        if with_q:
            _q_tail(xn, rows, *q_in, *q_out)

    x, h = prologue(0)
    pending = None
    for r in range(n_sub):
        y = None
        for ci, k in enumerate(chunks):
            if ci == len(chunks) - 1 and r + 1 < n_sub:
                nxt = prologue(r + 1)
            y = ff_chunk(h, k, y)
            if ci == 0 and pending is not None:
                epilogue(*pending)
                pending = None
        pending = (r, x, y)
        if r + 1 < n_sub:
            x, h = nxt
    epilogue(*pending)


def _mlp_layer(x, mod, g, w_up, w_down, oproj=None, kv=None, q=None, casts=()):
    bsz, s, d = x.shape
    tm = TOKEN_TILE
    grid = (bsz, s // tm)
    c_args, c_in, c_out, c_shapes = _cast_jobs(casts, grid)
    row = lambda n: pl.BlockSpec((1, tm, n), lambda b, i: (b, i, 0))
    vec = lambda n: pl.BlockSpec((1, 1, n), lambda b, i: (b, 0, 0))
    table = lambda n: pl.BlockSpec((tm, n), lambda b, i: (i, 0))
    heads = lambda n: pl.BlockSpec((1, N_HEADS, tm, n), lambda b, i: (b, 0, i, 0))
    in_specs = [row(d), vec(6 * d), _const_spec(g.shape),
                _const_spec(w_up.shape), _const_spec(w_down.shape)]
    args = [x, mod, g, w_up, w_down]
    out_specs = [row(d)]
    out_shape = [jax.ShapeDtypeStruct(x.shape, F32)]
    name = "mlp"
    if oproj is not None:
        attn, w_o = oproj
        in_specs += [heads(attn.shape[3]), _const_spec(w_o.shape)]
        args += [attn, w_o]
        name = "oproj_" + name
    if kv is not None:
        assert len(kv) == N_KV_IN
        kv_mod, kv_g, w_a, g_a, w_k, w_vt, rope_tab = kv
        n_v = w_vt.shape[0]
        in_specs += [vec(2 * d), _const_spec((1, d)), _const_spec(w_a.shape),
                     _const_spec((1, KV_LORA)), _const_spec(w_k.shape),
                     _const_spec(w_vt.shape), table(LANES)]
        args += [kv_mod, kv_g.reshape(1, d), w_a, g_a.reshape(1, KV_LORA), w_k,
                 w_vt, rope_tab]
        out_specs += [heads(QK_NOPE),
                      pl.BlockSpec((1, n_v, tm), lambda b, i: (b, 0, i)),
                      row(LANES)]
        out_shape += [jax.ShapeDtypeStruct((bsz, N_HEADS, s, QK_NOPE), BF16),
                      jax.ShapeDtypeStruct((bsz, n_v, s), BF16),
                      jax.ShapeDtypeStruct((bsz, s, LANES), BF16)]
        name += "_kv"
    if q is not None:
        assert len(q) == N_Q_IN
        q_mod, q_g, w_a, g_a, w_b, rope_tab = q
        in_specs += [vec(6 * d), _const_spec(q_g.shape), _const_spec(w_a.shape),
                     _const_spec((1, Q_LORA)), _const_spec(w_b.shape),
                     pl.BlockSpec((2 * QK_ROPE, tm), lambda b, i: (0, i))]
        args += [q_mod, q_g, w_a, g_a.reshape(1, Q_LORA), w_b, rope_tab]
        out_specs += [pl.BlockSpec((1, N_HEADS, 2 * LANES, tm),
                                   lambda b, i: (b, 0, 0, i))]
        out_shape += [jax.ShapeDtypeStruct((bsz, N_HEADS, 2 * LANES, s), BF16)]
        name += "_q"
    return pl.pallas_call(
        functools.partial(_mlp_kernel, with_oproj=oproj is not None,
                          with_kv=kv is not None, with_q=q is not None,
                          n_casts=len(c_args)),
        out_shape=(*out_shape, *c_shapes),
        grid=grid,
        in_specs=in_specs + c_in,
        out_specs=(*out_specs, *c_out),
        compiler_params=_params(2),
        name=name,
    )(*args, *c_args)


def _col_reduce(x, op, final):
    n = x.shape[0]
    while n > 8 and n % 16 == 0:
        n //= 2
        x = op(x[:n], x[n:])
    return final(x, axis=0, keepdims=True)


def _attn_kernel(q_ref, kn_ref, kpe_ref, vt_ref, o_ref, kc_ref, s_ref, p_ref):
    s = q_ref.shape[3]
    tq = Q_TILE
    for hh in range(HEADS_PER_STEP):
        kc_ref[hh, :, 0:LANES] = kn_ref[0, hh]
        kc_ref[hh, :, LANES:2 * LANES] = kpe_ref[0]
    k_chunk = lax.broadcasted_iota(jnp.int32, (tq, tq), 0) // CHUNK
    q_chunk = lax.broadcasted_iota(jnp.int32, (tq, tq), 1) // CHUNK
    diag_mask = k_chunk <= q_chunk
    neg = jnp.finfo(F32).min
    n_tiles = s // tq

    def scores(item, slot):
        hh, i = item
        lo, hi = i * tq, (i + 1) * tq
        q = q_ref[0, hh, :, lo:hi]
        s_d = jnp.where(diag_mask, _dot(kc_ref[hh, lo:hi, :], q), neg)
        s_ref[slot, lo:hi, :] = s_d
        m = _col_reduce(s_d, jnp.maximum, jnp.max)
        for a in range(0, lo, tq):
            s_o = _dot(kc_ref[hh, a:a + tq, :], q)
            s_ref[slot, a:a + tq, :] = s_o
            m = jnp.maximum(m, _col_reduce(s_o, jnp.maximum, jnp.max))
        return m

    def softmax_pv(item, slot, m):
        hh, i = item
        lo, hi = i * tq, (i + 1) * tq
        blocks = [(j * tq, (j + 1) * tq) for j in range(i + 1)]
        l = None
        for a, b in blocks:
            p = jnp.exp2(s_ref[slot, a:b, :] - m)
            lb = _col_reduce(p, jnp.add, jnp.sum)
            l = lb if l is None else l + lb
            p_ref[slot, a:b, :] = p.astype(BF16)
        acc = _dot(vt_ref[0, hh * V_HEAD:(hh + 1) * V_HEAD, 0:hi],
                   p_ref[slot, 0:hi, :])
        o_ref[0, hh, lo:hi, :] = (acc * (1.0 / l)).T.astype(BF16)

    half = HEADS_PER_STEP // 2
    items = ([(hh, i) for i in range(n_tiles) for hh in range(half)]
             + [(hh, i) for i in reversed(range(n_tiles))
                for hh in range(half, HEADS_PER_STEP)])
    m_next = scores(items[0], 0)
    for pos, item in enumerate(items):
        m = m_next
        if pos + 1 < len(items):
            m_next = scores(items[pos + 1], (pos + 1) % 2)
        softmax_pv(item, pos % 2, m)


def _attention(q_cat, k_nope, k_pe, v_t):
    bsz, _, _, s = q_cat.shape
    hps = HEADS_PER_STEP
    assert N_HEADS % hps == 0
    head = lambda n: pl.BlockSpec((1, hps, s, n), lambda b, h: (b, h, 0, 0))
    return pl.pallas_call(
        _attn_kernel,
        out_shape=jax.ShapeDtypeStruct((bsz, N_HEADS, s, V_HEAD), BF16),
        grid=(bsz, N_HEADS // hps),
        in_specs=[
            pl.BlockSpec((1, hps, 2 * LANES, s), lambda b, h: (b, h, 0, 0)),
            head(QK_NOPE),
            pl.BlockSpec((1, s, LANES), lambda b, h: (b, 0, 0)),
            pl.BlockSpec((1, hps * V_HEAD, s), lambda b, h: (b, h, 0)),
        ],
        out_specs=head(V_HEAD),
        scratch_shapes=[pltpu.VMEM((hps, s, 2 * LANES), BF16),
                        pltpu.VMEM((2, s, Q_TILE), F32),
                        pltpu.VMEM((2, s, Q_TILE), BF16)],
        compiler_params=_params(2),
        name="mla_attention",
    )(q_cat, k_nope, k_pe, v_t)


def _half_swap(w):
    half = w.shape[-1] // 2
    return jnp.concatenate([w[..., half:], w[..., :half]], axis=-1)


def _rope_tables(seq_len):
    inv_freq = 1.0 / (ROPE_THETA ** (jnp.arange(0, QK_ROPE, 2, dtype=F32) / QK_ROPE))
    ang_t = inv_freq[:, None] * jnp.arange(seq_len, dtype=F32)[None, :]
    cos_t, sin_t = jnp.cos(ang_t), jnp.sin(ang_t)
    tab_t = jnp.concatenate([cos_t, cos_t, -sin_t, sin_t], axis=0)
    return tab_t.T, tab_t * Q_SCALE


def kernel(x, c, ada_w, ada_b, norm_g, conv_in_w, conv_w, conv_b, conv_out_w,
           kv_ada_w, kv_ada_b, kv_norm_g, kv_a_w, kv_a_norm_g, kv_b_w,
           q_a_w, q_a_norm_g, q_b_w, attn_o_w, mlp_up_w, mlp_down_w):
    bsz, s, d = x.shape
    depth = ada_w.shape[0]
    n_a = conv_in_w.shape[0]
    assert 1 <= n_a < depth, "needs a conv layer before the first attention layer"

    mod = _ada(c, ada_w, ada_b).reshape(depth, bsz, 1, 6 * d)
    kv_mod = _ada(c, kv_ada_w[None], kv_ada_b[None]).reshape(bsz, 1, 2 * d)
    k_tab, q_tab = _rope_tables(s)

    w_pe = kv_a_w[:, KV_LORA:]
    w_kva = jnp.concatenate([kv_a_w[:, :KV_LORA], w_pe, _half_swap(w_pe)], axis=1)
    w_kvb = kv_b_w.reshape(KV_LORA, N_HEADS, QK_NOPE + V_HEAD)
    w_k = w_kvb[:, :, :QK_NOPE].reshape(KV_LORA, -1)
    w_vt = w_kvb[:, :, QK_NOPE:].reshape(KV_LORA, -1).T
    kv_args = (kv_mod, kv_norm_g, w_kva.astype(BF16), kv_a_norm_g,
               w_k.astype(BF16), w_vt.astype(BF16), k_tab)

    def q_args(l):
        j = l - n_a
        w_qb = q_b_w[j].reshape(Q_LORA, N_HEADS, QK_NOPE + QK_ROPE)
        w_qb = jnp.concatenate([w_qb, _half_swap(w_qb[:, :, QK_NOPE:])], axis=-1)
        w_qbt = w_qb.reshape(Q_LORA, N_HEADS * 2 * LANES).T
        return (mod[l], norm_g[l], q_a_w[j].astype(BF16), q_a_norm_g[j],
                w_qbt.astype(BF16), q_tab)

    def mixer_weights(l):
        if l < n_a:
            return [(conv_in_w, l), (conv_out_w, l)]
        return [(attn_o_w, l - n_a)]

    mix_w = [w[l].astype(BF16) for w, l in mixer_weights(0)]
    k_nope = v_t = k_pe = q_cat = None
    for l in range(depth):
        mlp_casts = [(mlp_up_w, l), (mlp_down_w, l)]
        next_casts = mixer_weights(l + 1) if l + 1 < depth else []
        oproj = None
        if l < n_a:
            x, w_up, w_down = _conv_layer(x, mod[l], norm_g[l], mix_w[0], conv_w[l],
                                          conv_b[l], mix_w[1], mlp_casts)
        else:
            attn = _attention(q_cat, k_nope, k_pe, v_t)
            oproj = (attn, mix_w[0])
            w_up, w_down = mlp_w
        if l + 1 >= n_a:
            next_casts = next_casts + ([(mlp_up_w, l + 1), (mlp_down_w, l + 1)]
                                       if l + 1 < depth else [])
        kv = kv_args if l + 1 == n_a else None
        q = q_args(l + 1) if n_a <= l + 1 < depth else None
        outs = _mlp_layer(x, mod[l], norm_g[l], w_up, w_down, oproj, kv, q,
                          next_casts)
        x = outs[0]
        if kv is not None:
            k_nope, v_t, k_pe = outs[1:4]
        if q is not None:
            q_cat = outs[4] if kv is not None else outs[1]
        cast_outs = list(outs[len(outs) - len(next_casts):]) if next_casts else []
        n_mix = len(mixer_weights(l + 1)) if l + 1 < depth else 0
        mix_w, mlp_w = cast_outs[:n_mix], cast_outs[n_mix:]
    return x
```

```python
import functools
import math

import jax
import jax.numpy as jnp
from jax import lax
from jax.experimental import pallas as pl
from jax.experimental.pallas import tpu as pltpu

CHUNK = 64
CONV_W = 3
N_HEADS = 8
QK_NOPE = 128
QK_ROPE = 64
V_HEAD = 128
Q_LORA = 384
KV_LORA = 256
ROPE_THETA = 10000.0
NORM_EPS = 1e-6

V7X_VMEM_LIMIT_BYTES = 56 * 1024 * 1024
LANES = 128

TOKEN_TILE = 512
CONV_TOKEN_TILE = 1024
FF_CHUNK = 2048
ADA_COL_TILE = 1536
Q_TILE = 256
ROW_SUB = 256
HEADS_PER_STEP = 4
SCORE_SLOTS = 3

F32 = jnp.float32
BF16 = jnp.bfloat16

Q_SCALE = math.log2(math.e) / math.sqrt(QK_NOPE + QK_ROPE)


def _normalize(x):
    ms = jnp.mean(x * x, axis=-1, keepdims=True)
    return x * lax.rsqrt(ms + NORM_EPS)


def _rms(x, g):
    return _normalize(x) * g


def _mod_gain(g, scale):
    return g * (1.0 + scale)


def _dot(a, b):
    return jnp.dot(a, b, preferred_element_type=F32)


def _dot_nt(a, b):
    return lax.dot_general(a, b, (((1,), (1,)), ((), ())),
                           preferred_element_type=F32)


def _const_spec(shape):
    nd = len(shape)
    return pl.BlockSpec(shape, lambda *_: (0,) * nd,
                        pipeline_mode=pl.Buffered(1))


def _params(n_axes):
    return pltpu.CompilerParams(
        dimension_semantics=("arbitrary",) * n_axes,
        vmem_limit_bytes=V7X_VMEM_LIMIT_BYTES)


def _cast_jobs(weights, grid):
    n_b, n_i = grid
    args, in_specs, out_specs, out_shapes = [], [], [], []
    for w, layer in weights:
        _, rows, cols = w.shape
        rb = rows // (n_b * n_i)
        assert rb * n_b * n_i == rows and rb % 16 == 0, (w.shape, grid)
        args.append(w)
        in_specs.append(pl.BlockSpec(
            (1, rb, cols), lambda b, i, layer=layer: (layer, b * n_i + i, 0)))
        out_specs.append(pl.BlockSpec((rb, cols), lambda b, i: (b * n_i + i, 0)))
        out_shapes.append(jax.ShapeDtypeStruct((rows, cols), BF16))
    return args, in_specs, out_specs, out_shapes


def _run_cast_jobs(src_refs, dst_refs):
    for src, dst in zip(src_refs, dst_refs):
        dst[...] = src[0].astype(BF16)


def _ada_kernel(c_ref, w_ref, b_ref, o_ref):
    c = c_ref[...]
    c_act = (c * jax.nn.sigmoid(c)).astype(BF16)
    o_ref[0] = _dot(c_act, w_ref[0].astype(BF16)) + b_ref[0]


def _ada(c, w, b):
    n_layers, d, n = w.shape
    bsz = c.shape[0]
    tn = max(t for t in range(LANES, ADA_COL_TILE + 1, LANES) if n % t == 0)
    return pl.pallas_call(
        _ada_kernel,
        out_shape=jax.ShapeDtypeStruct((n_layers, bsz, n), F32),
        grid=(n_layers, n // tn),
        in_specs=[
            _const_spec((bsz, d)),
            pl.BlockSpec((1, d, tn), lambda l, j: (l, 0, j)),
            pl.BlockSpec((1, 1, tn), lambda l, j: (l, 0, j)),
        ],
        out_specs=pl.BlockSpec((1, bsz, tn), lambda l, j: (l, 0, j)),
        compiler_params=_params(2),
        name="ada_mod",
    )(c, w, b.reshape(n_layers, 1, n))


def _conv_kernel(*refs, n_casts):
    x_ref, mod_ref, g_ref, win_ref, cw_ref, cb_ref, wout_ref = refs[:7]
    cast_src = refs[7:7 + n_casts]
    o_ref = refs[7 + n_casts]
    cast_dst = refs[8 + n_casts:8 + 2 * n_casts]
    u_ref = refs[8 + 2 * n_casts]
    _run_cast_jobs(cast_src, cast_dst)
    tm, d = x_ref.shape[1], x_ref.shape[2]
    mod = mod_ref[0]
    shift, gate = mod[:, 0:d], mod[:, 2 * d:3 * d]
    gain = _mod_gain(g_ref[0:1, :], mod[:, d:2 * d])

    @pl.when(pl.program_id(1) == 0)
    def _():
        u_ref[0:8, :] = jnp.zeros((8, d), F32)

    @pl.when(pl.program_id(1) > 0)
    def _():
        u_ref[0:8, :] = u_ref[tm:tm + 8, :]

    x = x_ref[0]
    hb = (_normalize(x) * gain + shift).astype(BF16)
    cx = _dot(hb, win_ref[:, d:3 * d])
    bg = _dot(hb, win_ref[:, 0:d])
    u = cx[:, 0:d] * cx[:, d:2 * d]
    u_ref[8:8 + tm, :] = u
    conv = (cw_ref[0:1, :] * u_ref[6:6 + tm, :]
            + cw_ref[1:2, :] * u_ref[7:7 + tm, :]
            + cw_ref[2:3, :] * u
            + cb_ref[...])
    y = _dot((bg * conv).astype(BF16), wout_ref[...])
    o_ref[0] = x + gate * _rms(y, g_ref[1:2, :])


def _conv_layer(x, mod, g, w_in, conv_w, conv_b, w_out, casts=()):
    bsz, s, d = x.shape
    tm = CONV_TOKEN_TILE
    grid = (bsz, s // tm)
    c_args, c_in, c_out, c_shapes = _cast_jobs(casts, grid)
    return pl.pallas_call(
        functools.partial(_conv_kernel, n_casts=len(c_args)),
        out_shape=(jax.ShapeDtypeStruct(x.shape, F32), *c_shapes),
        grid=grid,
        in_specs=[
            pl.BlockSpec((1, tm, d), lambda b, i: (b, i, 0)),
            pl.BlockSpec((1, 1, 6 * d), lambda b, i: (b, 0, 0)),
            _const_spec(g.shape),
            _const_spec(w_in.shape),
            _const_spec(conv_w.shape),
            _const_spec((1, d)),
            _const_spec(w_out.shape),
            *c_in,
        ],
        out_specs=(pl.BlockSpec((1, tm, d), lambda b, i: (b, i, 0)), *c_out),
        scratch_shapes=[pltpu.VMEM((tm + 8, d), F32)],
        compiler_params=_params(2),
        name="conv_mixer",
    )(x, mod, g, w_in, conv_w, conv_b.reshape(1, d), w_out, *c_args)


def _kv_tail(xn, rows, mod_ref, g_ref, wa_ref, ga_ref, wk_ref, wvt_ref, tab_ref,
             kn_ref, vt_ref, kpe_ref):
    d = xn.shape[1]
    mod = mod_ref[0]
    h = xn * _mod_gain(g_ref[...], mod[:, d:2 * d]) + mod[:, 0:d]
    kva = _dot(h.astype(BF16), wa_ref[...])
    c_kv = _rms(kva[:, 0:KV_LORA], ga_ref[...]).astype(BF16)
    pe = kva[:, KV_LORA:KV_LORA + LANES]
    prod = pe * tab_ref[rows, :]
    lane = lax.broadcasted_iota(jnp.int32, prod.shape, 1)
    kpe = jnp.where(lane < QK_ROPE, prod + pltpu.roll(prod, QK_ROPE, 1), 0.0)
    kpe_ref[0, rows, :] = kpe.astype(BF16)
    kn = _dot(c_kv, wk_ref[...]).astype(BF16)
    for hd in range(N_HEADS):
        kn_ref[0, hd, rows, :] = kn[:, hd * QK_NOPE:(hd + 1) * QK_NOPE]
    vt_ref[0, :, rows] = _dot_nt(wvt_ref[...], c_kv).astype(BF16)


def _q_tail(xn, rows, mod_ref, g_ref, wa_ref, ga_ref, wbt_ref, tab_ref, q_ref):
    d = xn.shape[1]
    n_rows = xn.shape[0]
    hw = 2 * LANES
    mod = mod_ref[0]
    h = xn * _mod_gain(g_ref[0:1, :], mod[:, d:2 * d]) + mod[:, 0:d]
    qa = _rms(_dot(h.astype(BF16), wa_ref[...]), ga_ref[...]).astype(BF16)
    half = N_HEADS * hw // 2
    qt = [_dot_nt(wbt_ref[0:half, :], qa), _dot_nt(wbt_ref[half:2 * half, :], qa)]
    tab = tab_ref[:, rows]
    cos_t, sin_t = tab[0:QK_ROPE, :], tab[QK_ROPE:2 * QK_ROPE, :]
    zeros = jnp.zeros((hw - QK_NOPE - QK_ROPE, n_rows), BF16)
    for hd in range(N_HEADS):
        part, r0 = qt[hd * hw // half], (hd * hw) % half
        blk = part[r0:r0 + hw, :]
        rope = (blk[QK_NOPE:QK_NOPE + QK_ROPE, :] * cos_t
                + blk[QK_NOPE + QK_ROPE:hw, :] * sin_t)
        q_ref[0, hd, 0:QK_NOPE, rows] = (blk[0:QK_NOPE, :] * Q_SCALE).astype(BF16)
        q_ref[0, hd, QK_NOPE:QK_NOPE + QK_ROPE, rows] = rope.astype(BF16)
        q_ref[0, hd, QK_NOPE + QK_ROPE:hw, rows] = zeros


N_KV_IN, N_KV_OUT = 7, 3
N_Q_IN, N_Q_OUT = 6, 1


def _mlp_kernel(*refs, with_oproj, with_kv, with_q, n_casts):
    refs = list(refs)
    x_ref, mod_ref, g_ref, up_ref, down_ref = refs[:5]
    pos = 5
    if with_oproj:
        a_ref, ow_ref = refs[pos:pos + 2]
        pos += 2
    if with_kv:
        kv_in = refs[pos:pos + N_KV_IN]
        pos += N_KV_IN
    if with_q:
        q_in = refs[pos:pos + N_Q_IN]
        pos += N_Q_IN
    cast_src = refs[pos:pos + n_casts]
    pos += n_casts
    o_ref = refs[pos]
    pos += 1
    if with_kv:
        kv_out = refs[pos:pos + N_KV_OUT]
        pos += N_KV_OUT
    if with_q:
        q_out = refs[pos:pos + N_Q_OUT]
        pos += N_Q_OUT
    cast_dst = refs[pos:pos + n_casts]
    _run_cast_jobs(cast_src, cast_dst)

    tm, d = x_ref.shape[1], x_ref.shape[2]
    d_ff = up_ref.shape[1]
    rs = ROW_SUB
    n_sub = tm // rs
    mod = mod_ref[0]
    gate_m = mod[:, 2 * d:3 * d]
    shift, gate = mod[:, 3 * d:4 * d], mod[:, 5 * d:6 * d]
    gain = _mod_gain(g_ref[2:3, :], mod[:, 4 * d:5 * d])
    chunks = list(range(0, d_ff, FF_CHUNK))

    def prologue(r):
        rows = slice(r * rs, (r + 1) * rs)
        x = x_ref[0, rows, :]
        if with_oproj:
            a = jnp.concatenate([a_ref[0, hd, rows, :] for hd in range(N_HEADS)],
                                axis=1)
            y = _dot(a, ow_ref[...])
            x = x + gate_m * _rms(y, g_ref[1:2, :])
        h = (_normalize(x) * gain + shift).astype(BF16)
        return x, h

    def ff_chunk(h, k, y):
        hid = _dot(h, up_ref[:, k:k + FF_CHUNK])
        act = jnp.square(jnp.maximum(hid, 0.0)).astype(BF16)
        part = _dot(act, down_ref[k:k + FF_CHUNK, :])
        return part if y is None else y + part

    def epilogue(r, x, y):
        rows = slice(r * rs, (r + 1) * rs)
        x_new = x + gate * _rms(y, g_ref[3:4, :])
        o_ref[0, rows, :] = x_new
        if with_kv or with_q:
            xn = _normalize(x_new)
        if with_kv:
            _kv_tail(xn, rows, *kv_in, *kv_out)
        if with_q:
            _q_tail(xn, rows, *q_in, *q_out)

    x, h = prologue(0)
    pending = None
    for r in range(n_sub):
        y = None
        for ci, k in enumerate(chunks):
            if ci == len(chunks) - 1 and r + 1 < n_sub:
                nxt = prologue(r + 1)
            y = ff_chunk(h, k, y)
            if ci == 0 and pending is not None:
                epilogue(*pending)
                pending = None
        pending = (r, x, y)
        if r + 1 < n_sub:
            x, h = nxt
    epilogue(*pending)


def _mlp_layer(x, mod, g, w_up, w_down, oproj=None, kv=None, q=None, casts=()):
    bsz, s, d = x.shape
    tm = TOKEN_TILE
    grid = (bsz, s // tm)
    c_args, c_in, c_out, c_shapes = _cast_jobs(casts, grid)
    row = lambda n: pl.BlockSpec((1, tm, n), lambda b, i: (b, i, 0))
    vec = lambda n: pl.BlockSpec((1, 1, n), lambda b, i: (b, 0, 0))
    table = lambda n: pl.BlockSpec((tm, n), lambda b, i: (i, 0))
    heads = lambda n: pl.BlockSpec((1, N_HEADS, tm, n), lambda b, i: (b, 0, i, 0))
    in_specs = [row(d), vec(6 * d), _const_spec(g.shape),
                _const_spec(w_up.shape), _const_spec(w_down.shape)]
    args = [x, mod, g, w_up, w_down]
    out_specs = [row(d)]
    out_shape = [jax.ShapeDtypeStruct(x.shape, F32)]
    name = "mlp"
    if oproj is not None:
        attn, w_o = oproj
        in_specs += [heads(attn.shape[3]), _const_spec(w_o.shape)]
        args += [attn, w_o]
        name = "oproj_" + name
    if kv is not None:
        assert len(kv) == N_KV_IN
        kv_mod, kv_g, w_a, g_a, w_k, w_vt, rope_tab = kv
        n_v = w_vt.shape[0]
        in_specs += [vec(2 * d), _const_spec((1, d)), _const_spec(w_a.shape),
                     _const_spec((1, KV_LORA)), _const_spec(w_k.shape),
                     _const_spec(w_vt.shape), table(LANES)]
        args += [kv_mod, kv_g.reshape(1, d), w_a, g_a.reshape(1, KV_LORA), w_k,
                 w_vt, rope_tab]
        out_specs += [heads(QK_NOPE),
                      pl.BlockSpec((1, n_v, tm), lambda b, i: (b, 0, i)),
                      row(LANES)]
        out_shape += [jax.ShapeDtypeStruct((bsz, N_HEADS, s, QK_NOPE), BF16),
                      jax.ShapeDtypeStruct((bsz, n_v, s), BF16),
                      jax.ShapeDtypeStruct((bsz, s, LANES), BF16)]
        name += "_kv"
    if q is not None:
        assert len(q) == N_Q_IN
        q_mod, q_g, w_a, g_a, w_b, rope_tab = q
        in_specs += [vec(6 * d), _const_spec(q_g.shape), _const_spec(w_a.shape),
                     _const_spec((1, Q_LORA)), _const_spec(w_b.shape),
                     pl.BlockSpec((2 * QK_ROPE, tm), lambda b, i: (0, i))]
        args += [q_mod, q_g, w_a, g_a.reshape(1, Q_LORA), w_b, rope_tab]
        out_specs += [pl.BlockSpec((1, N_HEADS, 2 * LANES, tm),
                                   lambda b, i: (b, 0, 0, i))]
        out_shape += [jax.ShapeDtypeStruct((bsz, N_HEADS, 2 * LANES, s), BF16)]
        name += "_q"
    return pl.pallas_call(
        functools.partial(_mlp_kernel, with_oproj=oproj is not None,
                          with_kv=kv is not None, with_q=q is not None,
                          n_casts=len(c_args)),
        out_shape=(*out_shape, *c_shapes),
        grid=grid,
        in_specs=in_specs + c_in,
        out_specs=(*out_specs, *c_out),
        compiler_params=_params(2),
        name=name,
    )(*args, *c_args)


def _col_reduce(x, op, final):
    n = x.shape[0]
    while n > 8 and n % 16 == 0:
        n //= 2
        x = op(x[:n], x[n:])
    return final(x, axis=0, keepdims=True)


def _attn_kernel(q_ref, kn_ref, kpe_ref, vt_ref, o_ref, kc_ref, s_ref, p_ref):
    s = q_ref.shape[3]
    tq = Q_TILE
    for hh in range(HEADS_PER_STEP):
        kc_ref[hh, :, 0:LANES] = kn_ref[0, hh]
        kc_ref[hh, :, LANES:2 * LANES] = kpe_ref[0]
    k_chunk = lax.broadcasted_iota(jnp.int32, (tq, tq), 0) // CHUNK
    q_chunk = lax.broadcasted_iota(jnp.int32, (tq, tq), 1) // CHUNK
    diag_mask = k_chunk <= q_chunk
    neg = jnp.finfo(F32).min
    n_tiles = s // tq

    def scores(item, slot):
        hh, i = item
        lo, hi = i * tq, (i + 1) * tq
        q = q_ref[0, hh, :, lo:hi]
        s_d = jnp.where(diag_mask, _dot(kc_ref[hh, lo:hi, :], q), neg)
        s_ref[slot, lo:hi, :] = s_d
        m = _col_reduce(s_d, jnp.maximum, jnp.max)
        for a in range(0, lo, tq):
            s_o = _dot(kc_ref[hh, a:a + tq, :], q)
            s_ref[slot, a:a + tq, :] = s_o
            m = jnp.maximum(m, _col_reduce(s_o, jnp.maximum, jnp.max))
        return m

    def softmax_pv(item, slot, m):
        hh, i = item
        lo, hi = i * tq, (i + 1) * tq
        blocks = [(j * tq, (j + 1) * tq) for j in range(i + 1)]
        l = None
        for a, b in blocks:
            p = jnp.exp2(s_ref[slot, a:b, :] - m)
            lb = _col_reduce(p, jnp.add, jnp.sum)
            l = lb if l is None else l + lb
            p_ref[slot, a:b, :] = p.astype(BF16)
        acc = _dot(vt_ref[0, hh * V_HEAD:(hh + 1) * V_HEAD, 0:hi],
                   p_ref[slot, 0:hi, :])
        o_ref[0, hh, lo:hi, :] = (acc * (1.0 / l)).T.astype(BF16)

    half = HEADS_PER_STEP // 2
    items = ([(hh, i) for i in range(n_tiles) for hh in range(half)]
             + [(hh, i) for i in reversed(range(n_tiles))
                for hh in range(half, HEADS_PER_STEP)])
    ahead = SCORE_SLOTS - 1
    maxes = {k: scores(items[k], k % SCORE_SLOTS) for k in range(ahead)}
    for pos, item in enumerate(items):
        nxt = pos + ahead
        if nxt < len(items):
            maxes[nxt] = scores(items[nxt], nxt % SCORE_SLOTS)
        softmax_pv(item, pos % SCORE_SLOTS, maxes.pop(pos))


def _attention(q_cat, k_nope, k_pe, v_t):
    bsz, _, _, s = q_cat.shape
    hps = HEADS_PER_STEP
    assert N_HEADS % hps == 0
    head = lambda n: pl.BlockSpec((1, hps, s, n), lambda b, h: (b, h, 0, 0))
    return pl.pallas_call(
        _attn_kernel,
        out_shape=jax.ShapeDtypeStruct((bsz, N_HEADS, s, V_HEAD), BF16),
        grid=(bsz, N_HEADS // hps),
        in_specs=[
            pl.BlockSpec((1, hps, 2 * LANES, s), lambda b, h: (b, h, 0, 0)),
            head(QK_NOPE),
            pl.BlockSpec((1, s, LANES), lambda b, h: (b, 0, 0)),
            pl.BlockSpec((1, hps * V_HEAD, s), lambda b, h: (b, h, 0)),
        ],
        out_specs=head(V_HEAD),
        scratch_shapes=[pltpu.VMEM((hps, s, 2 * LANES), BF16),
                        pltpu.VMEM((SCORE_SLOTS, s, Q_TILE), F32),
                        pltpu.VMEM((SCORE_SLOTS, s, Q_TILE), BF16)],
        compiler_params=_params(2),
        name="mla_attention",
    )(q_cat, k_nope, k_pe, v_t)


def _half_swap(w):
    half = w.shape[-1] // 2
    return jnp.concatenate([w[..., half:], w[..., :half]], axis=-1)


def _rope_tables(seq_len):
    inv_freq = 1.0 / (ROPE_THETA ** (jnp.arange(0, QK_ROPE, 2, dtype=F32) / QK_ROPE))
    ang_t = inv_freq[:, None] * jnp.arange(seq_len, dtype=F32)[None, :]
    cos_t, sin_t = jnp.cos(ang_t), jnp.sin(ang_t)
    tab_t = jnp.concatenate([cos_t, cos_t, -sin_t, sin_t], axis=0)
    return tab_t.T, tab_t * Q_SCALE


def kernel(x, c, ada_w, ada_b, norm_g, conv_in_w, conv_w, conv_b, conv_out_w,
           kv_ada_w, kv_ada_b, kv_norm_g, kv_a_w, kv_a_norm_g, kv_b_w,
           q_a_w, q_a_norm_g, q_b_w, attn_o_w, mlp_up_w, mlp_down_w):
    bsz, s, d = x.shape
    depth = ada_w.shape[0]
    n_a = conv_in_w.shape[0]
    assert 1 <= n_a < depth, "needs a conv layer before the first attention layer"

    mod = _ada(c, ada_w, ada_b).reshape(depth, bsz, 1, 6 * d)
    kv_mod = _ada(c, kv_ada_w[None], kv_ada_b[None]).reshape(bsz, 1, 2 * d)
    k_tab, q_tab = _rope_tables(s)

    w_pe = kv_a_w[:, KV_LORA:]
    w_kva = jnp.concatenate([kv_a_w[:, :KV_LORA], w_pe, _half_swap(w_pe)], axis=1)
    w_kvb = kv_b_w.reshape(KV_LORA, N_HEADS, QK_NOPE + V_HEAD)
    w_k = w_kvb[:, :, :QK_NOPE].reshape(KV_LORA, -1)
    w_vt = w_kvb[:, :, QK_NOPE:].reshape(KV_LORA, -1).T
    kv_args = (kv_mod, kv_norm_g, w_kva.astype(BF16), kv_a_norm_g,
               w_k.astype(BF16), w_vt.astype(BF16), k_tab)

    def q_args(l):
        j = l - n_a
        w_qb = q_b_w[j].reshape(Q_LORA, N_HEADS, QK_NOPE + QK_ROPE)
        w_qb = jnp.concatenate([w_qb, _half_swap(w_qb[:, :, QK_NOPE:])], axis=-1)
        w_qbt = w_qb.reshape(Q_LORA, N_HEADS * 2 * LANES).T
        return (mod[l], norm_g[l], q_a_w[j].astype(BF16), q_a_norm_g[j],
                w_qbt.astype(BF16), q_tab)

    def mixer_weights(l):
        if l < n_a:
            return [(conv_in_w, l), (conv_out_w, l)]
        return [(attn_o_w, l - n_a)]

    mix_w = [w[l].astype(BF16) for w, l in mixer_weights(0)]
    k_nope = v_t = k_pe = q_cat = None
    for l in range(depth):
        mlp_casts = [(mlp_up_w, l), (mlp_down_w, l)]
        next_casts = mixer_weights(l + 1) if l + 1 < depth else []
        oproj = None
        if l < n_a:
            x, w_up, w_down = _conv_layer(x, mod[l], norm_g[l], mix_w[0], conv_w[l],
                                          conv_b[l], mix_w[1], mlp_casts)
        else:
            attn = _attention(q_cat, k_nope, k_pe, v_t)
            oproj = (attn, mix_w[0])
            w_up, w_down = mlp_w
        if l + 1 >= n_a:
            next_casts = next_casts + ([(mlp_up_w, l + 1), (mlp_down_w, l + 1)]
                                       if l + 1 < depth else [])
        kv = kv_args if l + 1 == n_a else None
        q = q_args(l + 1) if n_a <= l + 1 < depth else None
        outs = _mlp_layer(x, mod[l], norm_g[l], w_up, w_down, oproj, kv, q,
                          next_casts)
        x = outs[0]
        if kv is not None:
            k_nope, v_t, k_pe = outs[1:4]
        if q is not None:
            q_cat = outs[4] if kv is not None else outs[1]
        cast_outs = list(outs[len(outs) - len(next_casts):]) if next_casts else []
        n_mix = len(mixer_weights(l + 1)) if l + 1 < depth else 0
        mix_w, mlp_w = cast_outs[:n_mix], cast_outs[n_mix:]
    return x
```

```python
import functools
import math

import jax
import jax.numpy as jnp
from jax import lax
from jax.experimental import pallas as pl
from jax.experimental.pallas import tpu as pltpu

CHUNK = 64
CONV_W = 3
N_HEADS = 8
QK_NOPE = 128
QK_ROPE = 64
V_HEAD = 128
Q_LORA = 384
KV_LORA = 256
ROPE_THETA = 10000.0
NORM_EPS = 1e-6

V7X_VMEM_LIMIT_BYTES = 56 * 1024 * 1024
LANES = 128

TOKEN_TILE = 512
CONV_TOKEN_TILE = 1024
FF_CHUNK = 2048
ADA_COL_TILE = 1536
Q_TILE = 256
ROW_SUB = 256
HEADS_PER_STEP = 4

F32 = jnp.float32
BF16 = jnp.bfloat16

Q_SCALE = math.log2(math.e) / math.sqrt(QK_NOPE + QK_ROPE)


def _normalize(x):
    ms = jnp.mean(x * x, axis=-1, keepdims=True)
    return x * lax.rsqrt(ms + NORM_EPS)


def _rms(x, g):
    return _normalize(x) * g


def _mod_gain(g, scale):
    return g * (1.0 + scale)


def _dot(a, b):
    return jnp.dot(a, b, preferred_element_type=F32)


def _dot_nt(a, b):
    return lax.dot_general(a, b, (((1,), (1,)), ((), ())),
                           preferred_element_type=F32)


def _const_spec(shape):
    nd = len(shape)
    return pl.BlockSpec(shape, lambda *_: (0,) * nd,
                        pipeline_mode=pl.Buffered(1))


def _params(n_axes):
    return pltpu.CompilerParams(
        dimension_semantics=("arbitrary",) * n_axes,
        vmem_limit_bytes=V7X_VMEM_LIMIT_BYTES)


def _cast_jobs(weights, grid):
    n_b, n_i = grid
    args, in_specs, out_specs, out_shapes = [], [], [], []
    for w, layer in weights:
        _, rows, cols = w.shape
        rb = rows // (n_b * n_i)
        assert rb * n_b * n_i == rows and rb % 16 == 0, (w.shape, grid)
        args.append(w)
        in_specs.append(pl.BlockSpec(
            (1, rb, cols), lambda b, i, layer=layer: (layer, b * n_i + i, 0)))
        out_specs.append(pl.BlockSpec((rb, cols), lambda b, i: (b * n_i + i, 0)))
        out_shapes.append(jax.ShapeDtypeStruct((rows, cols), BF16))
    return args, in_specs, out_specs, out_shapes


def _run_cast_jobs(src_refs, dst_refs):
    for src, dst in zip(src_refs, dst_refs):
        dst[...] = src[0].astype(BF16)


def _ada_kernel(c_ref, w_ref, b_ref, o_ref):
    c = c_ref[...]
    c_act = (c * jax.nn.sigmoid(c)).astype(BF16)
    o_ref[0] = _dot(c_act, w_ref[0].astype(BF16)) + b_ref[0]


def _ada(c, w, b):
    n_layers, d, n = w.shape
    bsz = c.shape[0]
    tn = max(t for t in range(LANES, ADA_COL_TILE + 1, LANES) if n % t == 0)
    return pl.pallas_call(
        _ada_kernel,
        out_shape=jax.ShapeDtypeStruct((n_layers, bsz, n), F32),
        grid=(n_layers, n // tn),
        in_specs=[
            _const_spec((bsz, d)),
            pl.BlockSpec((1, d, tn), lambda l, j: (l, 0, j)),
            pl.BlockSpec((1, 1, tn), lambda l, j: (l, 0, j)),
        ],
        out_specs=pl.BlockSpec((1, bsz, tn), lambda l, j: (l, 0, j)),
        compiler_params=_params(2),
        name="ada_mod",
    )(c, w, b.reshape(n_layers, 1, n))


def _conv_kernel(*refs, n_casts):
    x_ref, mod_ref, g_ref, win_ref, cw_ref, cb_ref, wout_ref = refs[:7]
    cast_src = refs[7:7 + n_casts]
    o_ref = refs[7 + n_casts]
    cast_dst = refs[8 + n_casts:8 + 2 * n_casts]
    u_ref = refs[8 + 2 * n_casts]
    _run_cast_jobs(cast_src, cast_dst)
    tm, d = x_ref.shape[1], x_ref.shape[2]
    mod = mod_ref[0]
    shift, gate = mod[:, 0:d], mod[:, 2 * d:3 * d]
    gain = _mod_gain(g_ref[0:1, :], mod[:, d:2 * d])

    @pl.when(pl.program_id(1) == 0)
    def _():
        u_ref[0:8, :] = jnp.zeros((8, d), F32)

    @pl.when(pl.program_id(1) > 0)
    def _():
        u_ref[0:8, :] = u_ref[tm:tm + 8, :]

    x = x_ref[0]
    hb = (_normalize(x) * gain + shift).astype(BF16)
    cx = _dot(hb, win_ref[:, d:3 * d])
    bg = _dot(hb, win_ref[:, 0:d])
    u = cx[:, 0:d] * cx[:, d:2 * d]
    u_ref[8:8 + tm, :] = u
    conv = (cw_ref[0:1, :] * u_ref[6:6 + tm, :]
            + cw_ref[1:2, :] * u_ref[7:7 + tm, :]
            + cw_ref[2:3, :] * u
            + cb_ref[...])
    y = _dot((bg * conv).astype(BF16), wout_ref[...])
    o_ref[0] = x + gate * _rms(y, g_ref[1:2, :])


def _conv_layer(x, mod, g, w_in, conv_w, conv_b, w_out, casts=()):
    bsz, s, d = x.shape
    tm = CONV_TOKEN_TILE
    grid = (bsz, s // tm)
    c_args, c_in, c_out, c_shapes = _cast_jobs(casts, grid)
    return pl.pallas_call(
        functools.partial(_conv_kernel, n_casts=len(c_args)),
        out_shape=(jax.ShapeDtypeStruct(x.shape, F32), *c_shapes),
        grid=grid,
        in_specs=[
            pl.BlockSpec((1, tm, d), lambda b, i: (b, i, 0)),
            pl.BlockSpec((1, 1, 6 * d), lambda b, i: (b, 0, 0)),
            _const_spec(g.shape),
            _const_spec(w_in.shape),
            _const_spec(conv_w.shape),
            _const_spec((1, d)),
            _const_spec(w_out.shape),
            *c_in,
        ],
        out_specs=(pl.BlockSpec((1, tm, d), lambda b, i: (b, i, 0)), *c_out),
        scratch_shapes=[pltpu.VMEM((tm + 8, d), F32)],
        compiler_params=_params(2),
        name="conv_mixer",
    )(x, mod, g, w_in, conv_w, conv_b.reshape(1, d), w_out, *c_args)


def _kv_tail(xn, rows, mod_ref, g_ref, wa_ref, ga_ref, wk_ref, wvt_ref, tab_ref,
             kn_ref, vt_ref, kpe_ref):
    d = xn.shape[1]
    mod = mod_ref[0]
    h = xn * _mod_gain(g_ref[...], mod[:, d:2 * d]) + mod[:, 0:d]
    kva = _dot(h.astype(BF16), wa_ref[...])
    c_kv = _rms(kva[:, 0:KV_LORA], ga_ref[...]).astype(BF16)
    pe = kva[:, KV_LORA:KV_LORA + LANES]
    prod = pe * tab_ref[rows, :]
    lane = lax.broadcasted_iota(jnp.int32, prod.shape, 1)
    kpe = jnp.where(lane < QK_ROPE, prod + pltpu.roll(prod, QK_ROPE, 1), 0.0)
    kpe_ref[0, rows, :] = kpe.astype(BF16)
    kn = _dot(c_kv, wk_ref[...]).astype(BF16)
    for hd in range(N_HEADS):
        kn_ref[0, hd, rows, :] = kn[:, hd * QK_NOPE:(hd + 1) * QK_NOPE]
    vt_ref[0, :, rows] = _dot_nt(wvt_ref[...], c_kv).astype(BF16)


def _q_tail(xn, rows, mod_ref, g_ref, wa_ref, ga_ref, wbt_ref, tab_ref, q_ref):
    d = xn.shape[1]
    n_rows = xn.shape[0]
    hw = 2 * LANES
    mod = mod_ref[0]
    h = xn * _mod_gain(g_ref[0:1, :], mod[:, d:2 * d]) + mod[:, 0:d]
    qa = _rms(_dot(h.astype(BF16), wa_ref[...]), ga_ref[...]).astype(BF16)
    half = N_HEADS * hw // 2
    qt = [_dot_nt(wbt_ref[0:half, :], qa), _dot_nt(wbt_ref[half:2 * half, :], qa)]
    tab = tab_ref[:, rows]
    cos_t, sin_t = tab[0:QK_ROPE, :], tab[QK_ROPE:2 * QK_ROPE, :]
    zeros = jnp.zeros((hw - QK_NOPE - QK_ROPE, n_rows), BF16)
    for hd in range(N_HEADS):
        part, r0 = qt[hd * hw // half], (hd * hw) % half
        blk = part[r0:r0 + hw, :]
        rope = (blk[QK_NOPE:QK_NOPE + QK_ROPE, :] * cos_t
                + blk[QK_NOPE + QK_ROPE:hw, :] * sin_t)
        q_ref[0, hd, 0:QK_NOPE, rows] = (blk[0:QK_NOPE, :] * Q_SCALE).astype(BF16)
        q_ref[0, hd, QK_NOPE:QK_NOPE + QK_ROPE, rows] = rope.astype(BF16)
        q_ref[0, hd, QK_NOPE + QK_ROPE:hw, rows] = zeros


N_KV_IN, N_KV_OUT = 7, 3
N_Q_IN, N_Q_OUT = 6, 1


def _mlp_kernel(*refs, with_oproj, with_kv, with_q, n_casts):
    refs = list(refs)
    x_ref, mod_ref, g_ref, up_ref, down_ref = refs[:5]
    pos = 5
    if with_oproj:
        a_ref, ow_ref = refs[pos:pos + 2]
        pos += 2
    if with_kv:
        kv_in = refs[pos:pos + N_KV_IN]
        pos += N_KV_IN
    if with_q:
        q_in = refs[pos:pos + N_Q_IN]
        pos += N_Q_IN
    cast_src = refs[pos:pos + n_casts]
    pos += n_casts
    o_ref = refs[pos]
    pos += 1
    if with_kv:
        kv_out = refs[pos:pos + N_KV_OUT]
        pos += N_KV_OUT
    if with_q:
        q_out = refs[pos:pos + N_Q_OUT]
        pos += N_Q_OUT
    cast_dst = refs[pos:pos + n_casts]
    _run_cast_jobs(cast_src, cast_dst)

    tm, d = x_ref.shape[1], x_ref.shape[2]
    d_ff = up_ref.shape[1]
    rs = ROW_SUB
    n_sub = tm // rs
    mod = mod_ref[0]
    gate_m = mod[:, 2 * d:3 * d]
    shift, gate = mod[:, 3 * d:4 * d], mod[:, 5 * d:6 * d]
    gain = _mod_gain(g_ref[2:3, :], mod[:, 4 * d:5 * d])
    chunks = list(range(0, d_ff, FF_CHUNK))

    def prologue(r):
        rows = slice(r * rs, (r + 1) * rs)
        x = x_ref[0, rows, :]
        if with_oproj:
            a = jnp.concatenate([a_ref[0, hd, rows, :] for hd in range(N_HEADS)],
                                axis=1)
            y = _dot(a, ow_ref[...])
            x = x + gate_m * _rms(y, g_ref[1:2, :])
        h = (_normalize(x) * gain + shift).astype(BF16)
        return x, h

    def ff_chunk(h, k, y):
        hid = _dot(h, up_ref[:, k:k + FF_CHUNK])
        act = jnp.square(jnp.maximum(hid, 0.0)).astype(BF16)
        part = _dot(act, down_ref[k:k + FF_CHUNK, :])
        return part if y is None else y + part

    def epilogue(r, x, y):
        rows = slice(r * rs, (r + 1) * rs)
        x_new = x + gate * _rms(y, g_ref[3:4, :])
        o_ref[0, rows, :] = x_new
        if with_kv or with_q:
            xn = _normalize(x_new)
        if with_kv:
            _kv_tail(xn, rows, *kv_in, *kv_out)
        if with_q:
            _q_tail(xn, rows, *q_in, *q_out)

    x, h = prologue(0)
    pending = None
    for r in range(n_sub):
        y = None
        for ci, k in enumerate(chunks):
            if ci == len(chunks) - 1 and r + 1 < n_sub:
                nxt = prologue(r + 1)
            y = ff_chunk(h, k, y)
            if ci == 0 and pending is not None:
                epilogue(*pending)
                pending = None
        pending = (r, x, y)
        if r + 1 < n_sub:
            x, h = nxt
    epilogue(*pending)


def _mlp_layer(x, mod, g, w_up, w_down, oproj=None, kv=None, q=None, casts=()):
    bsz, s, d = x.shape
    tm = TOKEN_TILE
    grid = (bsz, s // tm)
    c_args, c_in, c_out, c_shapes = _cast_jobs(casts, grid)
    row = lambda n: pl.BlockSpec((1, tm, n), lambda b, i: (b, i, 0))
    vec = lambda n: pl.BlockSpec((1, 1, n), lambda b, i: (b, 0, 0))
    table = lambda n: pl.BlockSpec((tm, n), lambda b, i: (i, 0))
    heads = lambda n: pl.BlockSpec((1, N_HEADS, tm, n), lambda b, i: (b, 0, i, 0))
    in_specs = [row(d), vec(6 * d), _const_spec(g.shape),
                _const_spec(w_up.shape), _const_spec(w_down.shape)]
    args = [x, mod, g, w_up, w_down]
    out_specs = [row(d)]
    out_shape = [jax.ShapeDtypeStruct(x.shape, F32)]
    name = "mlp"
    if oproj is not None:
        attn, w_o = oproj
        in_specs += [heads(attn.shape[3]), _const_spec(w_o.shape)]
        args += [attn, w_o]
        name = "oproj_" + name
    if kv is not None:
        assert len(kv) == N_KV_IN
        kv_mod, kv_g, w_a, g_a, w_k, w_vt, rope_tab = kv
        n_v = w_vt.shape[0]
        in_specs += [vec(2 * d), _const_spec((1, d)), _const_spec(w_a.shape),
                     _const_spec((1, KV_LORA)), _const_spec(w_k.shape),
                     _const_spec(w_vt.shape), table(LANES)]
        args += [kv_mod, kv_g.reshape(1, d), w_a, g_a.reshape(1, KV_LORA), w_k,
                 w_vt, rope_tab]
        out_specs += [heads(QK_NOPE),
                      pl.BlockSpec((1, n_v, tm), lambda b, i: (b, 0, i)),
                      row(LANES)]
        out_shape += [jax.ShapeDtypeStruct((bsz, N_HEADS, s, QK_NOPE), BF16),
                      jax.ShapeDtypeStruct((bsz, n_v, s), BF16),
                      jax.ShapeDtypeStruct((bsz, s, LANES), BF16)]
        name += "_kv"
    if q is not None:
        assert len(q) == N_Q_IN
        q_mod, q_g, w_a, g_a, w_b, rope_tab = q
        in_specs += [vec(6 * d), _const_spec(q_g.shape), _const_spec(w_a.shape),
                     _const_spec((1, Q_LORA)), _const_spec(w_b.shape),
                     pl.BlockSpec((2 * QK_ROPE, tm), lambda b, i: (0, i))]
        args += [q_mod, q_g, w_a, g_a.reshape(1, Q_LORA), w_b, rope_tab]
        out_specs += [pl.BlockSpec((1, N_HEADS, 2 * LANES, tm),
                                   lambda b, i: (b, 0, 0, i))]
        out_shape += [jax.ShapeDtypeStruct((bsz, N_HEADS, 2 * LANES, s), BF16)]
        name += "_q"
    return pl.pallas_call(
        functools.partial(_mlp_kernel, with_oproj=oproj is not None,
                          with_kv=kv is not None, with_q=q is not None,
                          n_casts=len(c_args)),
        out_shape=(*out_shape, *c_shapes),
        grid=grid,
        in_specs=in_specs + c_in,
        out_specs=(*out_specs, *c_out),
        compiler_params=_params(2),
        name=name,
    )(*args, *c_args)


def _col_reduce(x, op, final):
    n = x.shape[0]
    while n > 8 and n % 16 == 0:
        n //= 2
        x = op(x[:n], x[n:])
    return final(x, axis=0, keepdims=True)


def _attn_kernel(q_ref, kn_ref, kpe_ref, vt_ref, o_ref, kc_ref, s_ref, p_ref,
                 bias_ref):
    s = q_ref.shape[3]
    tq = Q_TILE
    for hh in range(HEADS_PER_STEP):
        kc_ref[hh, :, 0:LANES] = kn_ref[0, hh]
        kc_ref[hh, :, LANES:2 * LANES] = kpe_ref[0]
    k_chunk = lax.broadcasted_iota(jnp.int32, (tq, tq), 0) // CHUNK
    q_chunk = lax.broadcasted_iota(jnp.int32, (tq, tq), 1) // CHUNK
    bias_ref[...] = jnp.where(k_chunk <= q_chunk, 0.0, jnp.finfo(F32).min)
    n_tiles = s // tq

    def scores(item, slot):
        hh, i = item
        lo, hi = i * tq, (i + 1) * tq
        q = q_ref[0, hh, :, lo:hi]
        s_d = _dot(kc_ref[hh, lo:hi, :], q) + bias_ref[...]
        s_ref[slot, lo:hi, :] = s_d
        m = _col_reduce(s_d, jnp.maximum, jnp.max)
        for a in range(0, lo, tq):
            s_o = _dot(kc_ref[hh, a:a + tq, :], q)
            s_ref[slot, a:a + tq, :] = s_o
            m = jnp.maximum(m, _col_reduce(s_o, jnp.maximum, jnp.max))
        return m

    def softmax_pv(item, slot, m):
        hh, i = item
        lo, hi = i * tq, (i + 1) * tq
        blocks = [(j * tq, (j + 1) * tq) for j in range(i + 1)]
        l = None
        for a, b in blocks:
            p = jnp.exp2(s_ref[slot, a:b, :] - m)
            lb = _col_reduce(p, jnp.add, jnp.sum)
            l = lb if l is None else l + lb
            p_ref[slot, a:b, :] = p.astype(BF16)
        acc = _dot(vt_ref[0, hh * V_HEAD:(hh + 1) * V_HEAD, 0:hi],
                   p_ref[slot, 0:hi, :])
        o_ref[0, hh, lo:hi, :] = (acc * (1.0 / l)).T.astype(BF16)

    half = HEADS_PER_STEP // 2
    items = ([(hh, i) for i in range(n_tiles) for hh in range(half)]
             + [(hh, i) for i in reversed(range(n_tiles))
                for hh in range(half, HEADS_PER_STEP)])
    m_next = scores(items[0], 0)
    for pos, item in enumerate(items):
        m = m_next
        if pos + 1 < len(items):
            m_next = scores(items[pos + 1], (pos + 1) % 2)
        softmax_pv(item, pos % 2, m)


def _attention(q_cat, k_nope, k_pe, v_t):
    bsz, _, _, s = q_cat.shape
    hps = HEADS_PER_STEP
    assert N_HEADS % hps == 0
    head = lambda n: pl.BlockSpec((1, hps, s, n), lambda b, h: (b, h, 0, 0))
    return pl.pallas_call(
        _attn_kernel,
        out_shape=jax.ShapeDtypeStruct((bsz, N_HEADS, s, V_HEAD), BF16),
        grid=(bsz, N_HEADS // hps),
        in_specs=[
            pl.BlockSpec((1, hps, 2 * LANES, s), lambda b, h: (b, h, 0, 0)),
            head(QK_NOPE),
            pl.BlockSpec((1, s, LANES), lambda b, h: (b, 0, 0)),
            pl.BlockSpec((1, hps * V_HEAD, s), lambda b, h: (b, h, 0)),
        ],
        out_specs=head(V_HEAD),
        scratch_shapes=[pltpu.VMEM((hps, s, 2 * LANES), BF16),
                        pltpu.VMEM((2, s, Q_TILE), F32),
                        pltpu.VMEM((2, s, Q_TILE), BF16),
                        pltpu.VMEM((Q_TILE, Q_TILE), F32)],
        compiler_params=_params(2),
        name="mla_attention",
    )(q_cat, k_nope, k_pe, v_t)


def _half_swap(w):
    half = w.shape[-1] // 2
    return jnp.concatenate([w[..., half:], w[..., :half]], axis=-1)


def _rope_tables(seq_len):
    inv_freq = 1.0 / (ROPE_THETA ** (jnp.arange(0, QK_ROPE, 2, dtype=F32) / QK_ROPE))
    ang_t = inv_freq[:, None] * jnp.arange(seq_len, dtype=F32)[None, :]
    cos_t, sin_t = jnp.cos(ang_t), jnp.sin(ang_t)
    tab_t = jnp.concatenate([cos_t, cos_t, -sin_t, sin_t], axis=0)
    return tab_t.T, tab_t * Q_SCALE


def kernel(x, c, ada_w, ada_b, norm_g, conv_in_w, conv_w, conv_b, conv_out_w,
           kv_ada_w, kv_ada_b, kv_norm_g, kv_a_w, kv_a_norm_g, kv_b_w,
           q_a_w, q_a_norm_g, q_b_w, attn_o_w, mlp_up_w, mlp_down_w):
    bsz, s, d = x.shape
    depth = ada_w.shape[0]
    n_a = conv_in_w.shape[0]
    assert 1 <= n_a < depth, "needs a conv layer before the first attention layer"

    mod = _ada(c, ada_w, ada_b).reshape(depth, bsz, 1, 6 * d)
    kv_mod = _ada(c, kv_ada_w[None], kv_ada_b[None]).reshape(bsz, 1, 2 * d)
    k_tab, q_tab = _rope_tables(s)

    w_pe = kv_a_w[:, KV_LORA:]
    w_kva = jnp.concatenate([kv_a_w[:, :KV_LORA], w_pe, _half_swap(w_pe)], axis=1)
    w_kvb = kv_b_w.reshape(KV_LORA, N_HEADS, QK_NOPE + V_HEAD)
    w_k = w_kvb[:, :, :QK_NOPE].reshape(KV_LORA, -1)
    w_vt = w_kvb[:, :, QK_NOPE:].reshape(KV_LORA, -1).T
    kv_args = (kv_mod, kv_norm_g, w_kva.astype(BF16), kv_a_norm_g,
               w_k.astype(BF16), w_vt.astype(BF16), k_tab)

    def q_args(l):
        j = l - n_a
        w_qb = q_b_w[j].reshape(Q_LORA, N_HEADS, QK_NOPE + QK_ROPE)
        w_qb = jnp.concatenate([w_qb, _half_swap(w_qb[:, :, QK_NOPE:])], axis=-1)
        w_qbt = w_qb.reshape(Q_LORA, N_HEADS * 2 * LANES).T
        return (mod[l], norm_g[l], q_a_w[j].astype(BF16), q_a_norm_g[j],
                w_qbt.astype(BF16), q_tab)

    def mixer_weights(l):
        if l < n_a:
            return [(conv_in_w, l), (conv_out_w, l)]
        return [(attn_o_w, l - n_a)]

    mix_w = [w[l].astype(BF16) for w, l in mixer_weights(0)]
    k_nope = v_t = k_pe = q_cat = None
    for l in range(depth):
        mlp_casts = [(mlp_up_w, l), (mlp_down_w, l)]
        next_casts = mixer_weights(l + 1) if l + 1 < depth else []
        oproj = None
        if l < n_a:
            x, w_up, w_down = _conv_layer(x, mod[l], norm_g[l], mix_w[0], conv_w[l],
                                          conv_b[l], mix_w[1], mlp_casts)
        else:
            attn = _attention(q_cat, k_nope, k_pe, v_t)
            oproj = (attn, mix_w[0])
            w_up, w_down = mlp_w
        if l + 1 >= n_a:
            next_casts = next_casts + ([(mlp_up_w, l + 1), (mlp_down_w, l + 1)]
                                       if l + 1 < depth else [])
        kv = kv_args if l + 1 == n_a else None
        q = q_args(l + 1) if n_a <= l + 1 < depth else None
        outs = _mlp_layer(x, mod[l], norm_g[l], w_up, w_down, oproj, kv, q,
                          next_casts)
        x = outs[0]
        if kv is not None:
            k_nope, v_t, k_pe = outs[1:4]
        if q is not None:
            q_cat = outs[4] if kv is not None else outs[1]
        cast_outs = list(outs[len(outs) - len(next_casts):]) if next_casts else []
        n_mix = len(mixer_weights(l + 1)) if l + 1 < depth else 0
        mix_w, mlp_w = cast_outs[:n_mix], cast_outs[n_mix:]
    return x
```
